```python
import jax, jax.numpy as jnp
from jax import lax
import numpy as np

D_MODEL = 1024
BATCH = 8
SEQ = 4096
DEPTH = 4

N_MIXERS = 2
N_A = (DEPTH + 1) // 2
N_B = DEPTH // 2
RW_HEAD_DIM = 64
RW_HEADS = D_MODEL // RW_HEAD_DIM
LORA_DECAY = 64
LORA_ICLR = 64
LORA_VRES = 32
LORA_GATE = 160
N_SHIFT = 6
LNX_EPS = 64e-5
SG_CHUNK = 128
SG_WIDTH = 2 * D_MODEL
SG_GROUPS = 16
SG_GROUP_DIM = SG_WIDTH // SG_GROUPS
FFN_HIDDEN = -(-8 * D_MODEL // (3 * 256)) * 256
LN_EPS = 1e-5
ALPHA = (2.0 * DEPTH) ** 0.25
BETA = (8.0 * DEPTH) ** -0.25

kernel_name = "hybrid_rwkv7_chunked_gmlp_deepnorm"


def layer_norm(x, g, b, eps=LN_EPS):
    xf = x.astype(jnp.float32)
    mean = jnp.mean(xf, axis=-1, keepdims=True)
    var = jnp.mean(jnp.square(xf - mean), axis=-1, keepdims=True)
    y = (xf - mean) * lax.rsqrt(var + eps)
    return (y * g.astype(jnp.float32) + b.astype(jnp.float32)).astype(x.dtype)


def wkv7_scan(r, w, k, v, a, b):
    bsz, _, n_heads, n = r.shape

    def step(state, inp):
        r_t, w_t, k_t, v_t, a_t, b_t = inp
        sa = jnp.einsum("bhvk,bhk->bhv", state, a_t)
        state = (state * w_t[:, :, None, :] + sa[..., None] * b_t[:, :, None, :]
                 + v_t[..., None] * k_t[:, :, None, :])
        return state, jnp.einsum("bhvk,bhk->bhv", state, r_t)

    s0 = jnp.zeros((bsz, n_heads, n, n), jnp.float32)
    seq_major = tuple(jnp.swapaxes(z, 0, 1) for z in (r, w, k, v, a, b))
    _, ys = lax.scan(step, s0, seq_major)
    return jnp.swapaxes(ys, 0, 1)


def rwkv7_time_mix(x, v_first, mu, w_rkv, w0, w1, w2, a0, a1, a2, v_res,
                   g1, g2, k_k, k_a, r_k, lnx_g, lnx_b, w_out):
    bsz, t, d = x.shape
    f32 = jnp.float32
    heads = lambda z: z.reshape(bsz, t, RW_HEADS, RW_HEAD_DIM)
    xx = jnp.pad(x, ((0, 0), (1, 0), (0, 0)))[:, :-1] - x
    xr, xw, xk, xv, xa, xg = [x + xx * mu[i] for i in range(N_SHIFT)]
    r, k, v = jnp.einsum("jbtd,jde->jbte", jnp.stack([xr, xk, xv]), w_rkv)
    w_log = -jax.nn.softplus(-(w0 + jnp.tanh(xw @ w1) @ w2)) - 0.5
    decay = jnp.exp(-jnp.exp(w_log.astype(f32)))
    a = jax.nn.sigmoid(a0 + (xa @ a1) @ a2)
    if v_res is None:
        v_first = v
    else:
        v0, v1, v2 = v_res
        v = v + (v_first - v) * jax.nn.sigmoid(v0 + (xv @ v1) @ v2)
    g = jax.nn.sigmoid(xg @ g1) @ g2
    kk = heads(k * k_k).astype(f32)
    kk = kk / jnp.maximum(jnp.sqrt(jnp.sum(kk * kk, axis=-1, keepdims=True)), 1e-12)
    k = k * (1 + (a - 1) * k_a)
    rh, kh, vh, ah = (heads(z).astype(f32) for z in (r, k, v, a))
    y = wkv7_scan(rh, heads(decay), kh, vh, -kk, kk * ah)
    mean = jnp.mean(y, axis=-1, keepdims=True)
    var = jnp.mean(jnp.square(y - mean), axis=-1, keepdims=True)
    y = ((y - mean) * lax.rsqrt(var + LNX_EPS)).reshape(bsz, t, d)
    y = y * lnx_g.astype(f32) + lnx_b.astype(f32)
    bonus = jnp.sum(rh * kh * r_k.astype(f32), axis=-1, keepdims=True) * vh
    out = (y + bonus.reshape(bsz, t, d)) * g.astype(f32)
    return out.astype(x.dtype) @ w_out, v_first


def spatial_gating(x, w_in, b_in, ln_g, ln_b, w_s, b_s, w_out):
    bsz, t, _ = x.shape
    z = jax.nn.gelu(x @ w_in + b_in)
    u, v = jnp.split(z, 2, axis=-1)
    v = layer_norm(v, ln_g, ln_b)
    mask = jnp.tril(jnp.ones((SG_CHUNK, SG_CHUNK), dtype=bool))
    ws = jnp.where(mask[None], w_s, jnp.zeros_like(w_s))
    vc = v.reshape(bsz, t // SG_CHUNK, SG_CHUNK, SG_GROUPS, SG_GROUP_DIM)
    mixed = jnp.einsum("gij,bnjgc->bnigc", ws, vc) + b_s.T[None, None, :, :, None]
    return (u * mixed.reshape(bsz, t, SG_WIDTH)) @ w_out


def swiglu(x, w_gate_up, w_down):
    gate, up = jnp.split(x @ w_gate_up, 2, axis=-1)
    return (jax.nn.silu(gate) * up) @ w_down


def setup_inputs(seed: int = 0) -> dict:
    key = jax.random.key(seed)
    ks = jax.random.split(key, 31)
    D, E, F, C, G = D_MODEL, SG_WIDTH, FFN_HIDDEN, SG_CHUNK, SG_GROUPS
    NV = max(N_A - 1, 0)

    def nrm(k, shape, scale):
        return scale * jax.random.normal(k, shape, jnp.float32)

    return {
        "x": nrm(ks[0], (BATCH, SEQ, D), 1.0),
        "rw_mu": jax.random.uniform(ks[1], (N_A, N_SHIFT, D), jnp.float32),
        "rw_w_rkv": nrm(ks[2], (N_A, 3, D, D), D ** -0.5),
        "rw_w0": jax.random.uniform(ks[3], (N_A, D), jnp.float32, -5.0, 1.0),
        "rw_w1": nrm(ks[4], (N_A, D, LORA_DECAY), D ** -0.5),
        "rw_w2": nrm(ks[5], (N_A, LORA_DECAY, D), 0.1 * LORA_DECAY ** -0.5),
        "rw_a0": nrm(ks[6], (N_A, D), 0.3),
        "rw_a1": nrm(ks[7], (N_A, D, LORA_ICLR), D ** -0.5),
        "rw_a2": nrm(ks[8], (N_A, LORA_ICLR, D), 0.3 * LORA_ICLR ** -0.5),
        "rw_v0": 0.5 + nrm(ks[9], (NV, D), 0.3),
        "rw_v1": nrm(ks[10], (NV, D, LORA_VRES), D ** -0.5),
        "rw_v2": nrm(ks[11], (NV, LORA_VRES, D), 0.3 * LORA_VRES ** -0.5),
        "rw_g1": nrm(ks[12], (N_A, D, LORA_GATE), D ** -0.5),
        "rw_g2": nrm(ks[13], (N_A, LORA_GATE, D), LORA_GATE ** -0.5),
        "rw_k_k": 0.85 + nrm(ks[14], (N_A, D), 0.02),
        "rw_k_a": 1.0 + nrm(ks[15], (N_A, D), 0.02),
        "rw_r_k": nrm(ks[16], (N_A, RW_HEADS, RW_HEAD_DIM), 0.1),
        "rw_lnx_g": 1.0 + nrm(ks[17], (N_A, D), 0.02),
        "rw_lnx_b": nrm(ks[18], (N_A, D), 0.02),
        "rw_w_out": nrm(ks[19], (N_A, D, D), BETA * D ** -0.5),
        "sg_w_in": nrm(ks[20], (N_B, D, 2 * E), D ** -0.5),
        "sg_b_in": nrm(ks[21], (N_B, 2 * E), 0.02),
        "sg_ln_g": 1.0 + nrm(ks[22], (N_B, E), 0.02),
        "sg_ln_b": nrm(ks[23], (N_B, E), 0.02),
        "sg_w_s": nrm(ks[24], (N_B, G, C, C), C ** -0.5),
        "sg_b_s": 1.0 + nrm(ks[25], (N_B, G, C), 0.02),
        "sg_w_out": nrm(ks[26], (N_B, E, D), BETA * E ** -0.5),
        "ff_w_gate_up": nrm(ks[27], (DEPTH, D, 2 * F), D ** -0.5),
        "ff_w_down": nrm(ks[28], (DEPTH, F, D), BETA * F ** -0.5),
        "ln_g": 1.0 + nrm(ks[29], (DEPTH, 2, D), 0.02),
        "ln_b": nrm(ks[30], (DEPTH, 2, D), 0.02),
    }


def reference(x, rw_mu, rw_w_rkv, rw_w0, rw_w1, rw_w2, rw_a0, rw_a1, rw_a2,
              rw_v0, rw_v1, rw_v2, rw_g1, rw_g2, rw_k_k, rw_k_a, rw_r_k,
              rw_lnx_g, rw_lnx_b, rw_w_out, sg_w_in, sg_b_in, sg_ln_g, sg_ln_b,
              sg_w_s, sg_b_s, sg_w_out, ff_w_gate_up, ff_w_down, ln_g, ln_b):
    v_first = None
    for i in range(DEPTH):
        j = i // N_MIXERS
        if i % N_MIXERS == 0:
            v_res = None if j == 0 else (rw_v0[j - 1], rw_v1[j - 1], rw_v2[j - 1])
            h, v_first = rwkv7_time_mix(
                x, v_first, rw_mu[j], rw_w_rkv[j], rw_w0[j], rw_w1[j], rw_w2[j],
                rw_a0[j], rw_a1[j], rw_a2[j], v_res, rw_g1[j], rw_g2[j],
                rw_k_k[j], rw_k_a[j], rw_r_k[j], rw_lnx_g[j], rw_lnx_b[j], rw_w_out[j])
        else:
            h = spatial_gating(x, sg_w_in[j], sg_b_in[j], sg_ln_g[j], sg_ln_b[j],
                               sg_w_s[j], sg_b_s[j], sg_w_out[j])
        x = layer_norm(ALPHA * x + h, ln_g[i, 0], ln_b[i, 0])
        x = layer_norm(ALPHA * x + swiglu(x, ff_w_gate_up[i], ff_w_down[i]),
                       ln_g[i, 1], ln_b[i, 1])
    return x
```

```python
import functools

import jax
import jax.numpy as jnp
from jax import lax
from jax.experimental import pallas as pl
from jax.experimental.pallas import tpu as pltpu

F32 = jnp.float32
BF16 = jnp.bfloat16

RW_HEAD_DIM = 64
LNX_EPS = 64e-5
LN_EPS = 1e-5
SG_CHUNK = 128

LANES = 128
MXU_COLS = 256
VMEM_LIMIT_BYTES = 56 * 1024 * 1024

WKV_CHUNK = 64
HEADS_PER_SLAB = LANES // RW_HEAD_DIM


def _dot(a, b):
    return jnp.dot(a, b, preferred_element_type=F32)


def _dot_nt(a, b):
    return lax.dot_general(a, b, (((1,), (1,)), ((), ())), preferred_element_type=F32)


def _dot_tn(a, b):
    return lax.dot_general(a, b, (((0,), (0,)), ((), ())), preferred_element_type=F32)


def _sigmoid(x):
    return 1.0 / (1.0 + jnp.exp(-x))


def _layer_norm(y, g, b):
    mean = jnp.mean(y, axis=-1, keepdims=True)
    d = y - mean
    var = jnp.mean(d * d, axis=-1, keepdims=True)
    return d * lax.rsqrt(var + LN_EPS) * g + b


def _split2(x):
    hi = x.astype(BF16)
    lo = (x - hi.astype(F32)).astype(BF16)
    return hi, lo


def _split3(x):
    hi = x.astype(BF16)
    r1 = x - hi.astype(F32)
    mid = r1.astype(BF16)
    lo = (r1 - mid.astype(F32)).astype(BF16)
    return hi, mid, lo


def _const_spec(shape):
    nd = len(shape)
    return pl.BlockSpec(shape, lambda *_: (0,) * nd)


def _params(n_parallel):
    return pltpu.CompilerParams(
        dimension_semantics=("arbitrary",) * n_parallel,
        vmem_limit_bytes=VMEM_LIMIT_BYTES,
    )


def _rwkv_pre_kernel(has_vres, tiles_per_seq, *refs):
    if has_vres:
        (x_ref, xp_ref, vf_ref, mu_ref, wrkv_ref, w0_ref, w1_ref, w2_ref, a0_ref, a1_ref,
         a2_ref, v0_ref, v1_ref, v2_ref, g1_ref, g2_ref,
         r_out, lw_out, k_out, v_out, al_out, g_out) = refs
    else:
        (x_ref, xp_ref, mu_ref, wrkv_ref, w0_ref, w1_ref, w2_ref, a0_ref, a1_ref,
         a2_ref, g1_ref, g2_ref,
         r_out, lw_out, k_out, v_out, al_out, g_out) = refs
    i = pl.program_id(0)
    x = x_ref[...]
    tm = x.shape[0]
    prev_row = xp_ref[7:8, :] * jnp.where(i % tiles_per_seq == 0, 0.0, 1.0)
    row = lax.broadcasted_iota(jnp.int32, x.shape, 0)
    shifted = jnp.where(row == 0, prev_row, pltpu.roll(x, 1, 0))
    xx = shifted - x

    def mix(j):
        return (x + xx * mu_ref[j:j + 1, :]).astype(BF16)

    r_out[...] = _dot(mix(0), wrkv_ref[0])
    wl = w0_ref[...] + _dot(jnp.tanh(_dot(mix(1), w1_ref[...])).astype(BF16), w2_ref[...])
    nz = -wl
    softplus = jnp.maximum(nz, 0.0) + jnp.log(1.0 + jnp.exp(-jnp.abs(nz)))
    lw_out[...] = -jnp.exp(-softplus - 0.5)
    k_out[...] = _dot(mix(2), wrkv_ref[1])
    xv = mix(3)
    v = _dot(xv, wrkv_ref[2])
    if has_vres:
        gate = _sigmoid(v0_ref[...] + _dot(_dot(xv, v1_ref[...]).astype(BF16), v2_ref[...]))
        v = v + (vf_ref[...] - v) * gate
    v_out[...] = v
    al_out[...] = _sigmoid(
        a0_ref[...] + _dot(_dot(mix(4), a1_ref[...]).astype(BF16), a2_ref[...]))
    g_out[...] = _dot(_sigmoid(_dot(mix(5), g1_ref[...])).astype(BF16), g2_ref[...])


def _rwkv_pre(x2, v_first, seq_len, mu, wrkv, w0, w1, w2, a0, a1, a2, vres, g1, g2, tm=256):
    n, d = x2.shape
    has_vres = vres is not None
    row_spec = pl.BlockSpec((tm, d), lambda i: (i, 0))
    prev_spec = pl.BlockSpec((8, d), lambda i: (jnp.maximum(i * (tm // 8) - 1, 0), 0))
    ops = [x2, x2]
    specs = [row_spec, prev_spec]
    if has_vres:
        ops.append(v_first)
        specs.append(row_spec)
    consts = [mu, wrkv, w0, w1, w2, a0, a1, a2]
    if has_vres:
        consts += list(vres)
    consts += [g1, g2]
    ops += consts
    specs += [_const_spec(c.shape) for c in consts]
    out = jax.ShapeDtypeStruct((n, d), F32)
    return pl.pallas_call(
        functools.partial(_rwkv_pre_kernel, has_vres, seq_len // tm),
        grid=(n // tm,),
        in_specs=specs,
        out_specs=[row_spec] * 6,
        out_shape=[out] * 6,
        compiler_params=_params(1),
        name="rwkv_pre",
    )(*ops)


def _wkv_kernel(r_ref, lw_ref, k_ref, v_ref, al_ref, kk_ref, ka_ref, rk_ref, lg_ref, lb_ref,
                o_ref, st_ref):
    c = WKV_CHUNK
    s2 = HEADS_PER_SLAB * c

    @pl.when(pl.program_id(1) == 0)
    def _():
        st_ref[...] = jnp.zeros_like(st_ref)

    row = lax.broadcasted_iota(jnp.int32, (s2, s2), 0)
    col = lax.broadcasted_iota(jnp.int32, (s2, s2), 1)
    t_row = row & (c - 1)
    t_col = col & (c - 1)
    tri_strict = t_row > t_col
    tri_incl = t_row >= t_col
    eye = jnp.where(row == col, 1.0, 0.0).astype(F32)
    head_sum = jnp.where((row // RW_HEAD_DIM) == (col // RW_HEAD_DIM), 1.0, 0.0).astype(BF16)
    crow = lax.broadcasted_iota(jnp.int32, (c, c), 0)
    ccol = lax.broadcasted_iota(jnp.int32, (c, c), 1)
    cum_mat = jnp.where(crow >= ccol, 1.0, 0.0).astype(BF16)
    first_head = lax.broadcasted_iota(jnp.int32, (c, LANES), 1) < RW_HEAD_DIM

    def stack(z):
        zb = z.astype(BF16)
        zero = jnp.zeros_like(zb)
        return jnp.concatenate(
            [jnp.where(first_head, zb, zero), jnp.where(first_head, zero, zb)], axis=0)

    def seg_sum(z):
        hi, lo = _split2(z)
        return _dot(hi, head_sum) + _dot(lo, head_sum)

    n_slabs = r_ref.shape[1] // LANES
    for p in range(n_slabs):
        sl = slice(p * LANES, (p + 1) * LANES)
        r = r_ref[:, sl]
        lw = lw_ref[:, sl]
        k = k_ref[:, sl]
        v = v_ref[:, sl]
        al = al_ref[:, sl]

        kk = k * kk_ref[:, sl]
        kk = kk / jnp.maximum(jnp.sqrt(seg_sum(kk * kk)), 1e-12)
        k2 = k * (1.0 + (al - 1.0) * ka_ref[:, sl])
        a = -kk
        b = kk * al

        hi, mid, lo = _split3(lw)
        cum = _dot(cum_mat, hi) + _dot(cum_mat, mid) + _dot(cum_mat, lo)
        w_incl = jnp.exp(cum)
        w_excl = jnp.exp(cum - lw)
        w_inv = jnp.exp(-cum)
        w_end = w_incl[c - 1:c, :]
        w_tail = w_inv * w_end

        a_s = stack(a * w_excl)
        r_s = stack(r * w_incl)
        b_s = stack(b * w_inv)
        k_s = stack(k2 * w_inv)
        v_s = stack(v)
        ar = jnp.concatenate([a_s, r_s], axis=0)
        scores = _dot_nt(ar, jnp.concatenate([b_s, k_s], axis=0))
        n_ab = jnp.where(tri_strict, scores[:s2, :s2], 0.0)
        n_ak = jnp.where(tri_strict, scores[:s2, s2:], 0.0)
        n_rb = jnp.where(tri_incl, scores[s2:, :s2], 0.0)
        n_rk = jnp.where(tri_incl, scores[s2:, s2:], 0.0)

        inv = eye + n_ab
        pw = n_ab
        for _ in range(c.bit_length() - 2):
            pwb = pw.astype(BF16)
            pw = _dot(pwb, pwb)
            inv = inv + _dot(inv.astype(BF16), pw.astype(BF16))

        state = st_ref[p]
        from_state = _dot_nt(ar, state.astype(BF16))
        rhs = from_state[:s2] + _dot(n_ak.astype(BF16), v_s)
        u = _dot(inv.astype(BF16), rhs.astype(BF16))
        uv = jnp.concatenate([u.astype(BF16), v_s], axis=0)
        y_s = from_state[s2:] + _dot(
            jnp.concatenate([n_rb, n_rk], axis=1).astype(BF16), uv)
        y = y_s[:c] + y_s[c:]

        bk_tail = jnp.concatenate([stack(b * w_tail), stack(k2 * w_tail)], axis=0)
        st_ref[p] = state * w_end + _dot_tn(uv, bk_tail)

        inv_n = 1.0 / RW_HEAD_DIM
        mean = seg_sum(y) * inv_n
        dlt = y - mean
        var = seg_sum(dlt * dlt) * inv_n
        yn = dlt * lax.rsqrt(var + LNX_EPS) * lg_ref[:, sl] + lb_ref[:, sl]
        bonus = seg_sum(r * k2 * rk_ref[:, sl]) * v
        o_ref[:, sl] = yn + bonus


def _wkv(r, lw, k, v, al, k_k, k_a, r_k, lnx_g, lnx_b, batch, seq_len):
    n, d = r.shape
    c = WKV_CHUNK
    nc = seq_len // c
    row_spec = pl.BlockSpec((c, d), lambda bi, ci: (bi * nc + ci, 0))
    vec_spec = pl.BlockSpec((1, d), lambda bi, ci: (0, 0))
    s2 = HEADS_PER_SLAB * c
    return pl.pallas_call(
        _wkv_kernel,
        grid=(batch, nc),
        in_specs=[row_spec] * 5 + [vec_spec] * 5,
        out_specs=row_spec,
        out_shape=jax.ShapeDtypeStruct((n, d), F32),
        scratch_shapes=[pltpu.VMEM((d // LANES, s2, LANES), F32)],
        compiler_params=_params(2),
        name="wkv",
    )(r, lw, k, v, al, k_k, k_a, r_k, lnx_g, lnx_b)


def _rwkv_out_kernel(alpha, x_ref, y_ref, g_ref, w_ref, lg_ref, lb_ref, o_ref):
    h = _dot((y_ref[...] * g_ref[...]).astype(BF16), w_ref[...])
    o_ref[...] = _layer_norm(alpha * x_ref[...] + h, lg_ref[...], lb_ref[...])


def _rwkv_out(x2, y, g, w_out, ln_g, ln_b, alpha, tm=512):
    n, d = x2.shape
    row_spec = pl.BlockSpec((tm, d), lambda i: (i, 0))
    return pl.pallas_call(
        functools.partial(_rwkv_out_kernel, alpha),
        grid=(n // tm,),
        in_specs=[row_spec] * 3 + [_const_spec(w_out.shape), _const_spec(ln_g.shape),
                                   _const_spec(ln_b.shape)],
        out_specs=row_spec,
        out_shape=jax.ShapeDtypeStruct((n, d), F32),
        compiler_params=_params(1),
        name="rwkv_out",
    )(x2, y, g, w_out, ln_g, ln_b)


def _sg_kernel(alpha, x_ref, win_ref, bin_ref, lng_ref, lnb_ref, ws_ref, bs_ref, wout_ref,
               lg_ref, lb_ref, o_ref, gated_ref):
    x = x_ref[...]
    tm = x.shape[0]
    e = wout_ref.shape[0]
    xb = x.astype(BF16)

    def gelu(z):
        return 0.5 * z * (1.0 + jnp.tanh(0.7978845608028654 * (z + 0.044715 * z * z * z)))

    v = gelu(_dot(xb, win_ref[:, e:]) + bin_ref[:, e:])
    v = _layer_norm(v, lng_ref[...], lnb_ref[...]).astype(BF16)
    n_groups = ws_ref.shape[0]
    gd = e // n_groups
    row = lax.broadcasted_iota(jnp.int32, (SG_CHUNK, SG_CHUNK), 0)
    col = lax.broadcasted_iota(jnp.int32, (SG_CHUNK, SG_CHUNK), 1)
    causal = row >= col
    groups_per_pass = max(1, MXU_COLS // gd)
    for g0 in range(0, n_groups, groups_per_pass):
        wide = slice(g0 * gd, (g0 + groups_per_pass) * gd)
        u = gelu(_dot(xb, win_ref[:, wide]) + bin_ref[:, wide])
        for gi in range(groups_per_pass):
            g = g0 + gi
            ws = jnp.where(causal, ws_ref[g], jnp.zeros((), BF16))
            cols = slice(g * gd, (g + 1) * gd)
            for n in range(tm // SG_CHUNK):
                rows = slice(n * SG_CHUNK, (n + 1) * SG_CHUNK)
                mixed = _dot(ws, v[rows, cols]) + bs_ref[g]
                gated_ref[rows, cols] = (u[rows, gi * gd:(gi + 1) * gd] * mixed).astype(BF16)
    h = _dot(gated_ref[...], wout_ref[...])
    o_ref[...] = _layer_norm(alpha * x + h, lg_ref[...], lb_ref[...])


def _spatial_gating(x2, w_in, b_in, ln_g, ln_b, w_s, b_s, w_out, lg, lb, alpha, tm=256):
    n, d = x2.shape
    e = w_out.shape[0]
    row_spec = pl.BlockSpec((tm, d), lambda i: (i, 0))
    consts = [w_in, b_in, ln_g, ln_b, w_s, b_s, w_out, lg, lb]
    return pl.pallas_call(
        functools.partial(_sg_kernel, alpha),
        grid=(n // tm,),
        in_specs=[row_spec] + [_const_spec(c.shape) for c in consts],
        out_specs=row_spec,
        out_shape=jax.ShapeDtypeStruct((n, d), F32),
        scratch_shapes=[pltpu.VMEM((tm, e), BF16)],
        compiler_params=_params(1),
        name="spatial_gating",
    )(x2, *consts)


def _ffn_kernel(alpha, f_chunk, x_ref, wgu_ref, wd_ref, lg_ref, lb_ref, o_ref, acc_ref):
    x = x_ref[...]
    xb = x.astype(BF16)
    f = wd_ref.shape[0]
    for ci in range(f // f_chunk):
        lo = ci * f_chunk
        gate = _dot(xb, wgu_ref[:, lo:lo + f_chunk])
        up = _dot(xb, wgu_ref[:, f + lo:f + lo + f_chunk])
        act = (gate * _sigmoid(gate) * up).astype(BF16)
        part = _dot(act, wd_ref[lo:lo + f_chunk, :])
        if ci == 0:
            acc_ref[...] = part
        else:
            acc_ref[...] += part
    o_ref[...] = _layer_norm(alpha * x + acc_ref[...], lg_ref[...], lb_ref[...])


def _ffn(x2, w_gate_up, w_down, lg, lb, alpha, tm=512, f_chunk=256):
    n, d = x2.shape
    row_spec = pl.BlockSpec((tm, d), lambda i: (i, 0))
    consts = [w_gate_up, w_down, lg, lb]
    return pl.pallas_call(
        functools.partial(_ffn_kernel, alpha, f_chunk),
        grid=(n // tm,),
        in_specs=[row_spec] + [_const_spec(c.shape) for c in consts],
        out_specs=row_spec,
        out_shape=jax.ShapeDtypeStruct((n, d), F32),
        scratch_shapes=[pltpu.VMEM((tm, d), F32)],
        compiler_params=_params(1),
        name="swiglu",
    )(x2, *consts)


def kernel(x, rw_mu, rw_w_rkv, rw_w0, rw_w1, rw_w2, rw_a0, rw_a1, rw_a2, rw_v0, rw_v1, rw_v2, rw_g1, rw_g2, rw_k_k, rw_k_a, rw_r_k, rw_lnx_g, rw_lnx_b, rw_w_out, sg_w_in, sg_b_in, sg_ln_g, sg_ln_b, sg_w_s, sg_b_s, sg_w_out, ff_w_gate_up, ff_w_down, ln_g, ln_b):
    batch, seq_len, d = x.shape
    depth = ff_w_gate_up.shape[0]
    alpha = (2.0 * depth) ** 0.25
    n = batch * seq_len
    x2 = x.reshape(n, d)
    bf = lambda w: w.astype(BF16)
    rowv = lambda p: p.reshape(1, -1)

    v_first = None
    for i in range(depth):
        j = i // 2
        lg, lb = rowv(ln_g[i, 0]), rowv(ln_b[i, 0])
        if i % 2 == 0:
            vres = None if j == 0 else (rowv(rw_v0[j - 1]), bf(rw_v1[j - 1]), bf(rw_v2[j - 1]))
            r, lw, k, v, al, g = _rwkv_pre(
                x2, v_first, seq_len, rw_mu[j], bf(rw_w_rkv[j]), rowv(rw_w0[j]), bf(rw_w1[j]),
                bf(rw_w2[j]), rowv(rw_a0[j]), bf(rw_a1[j]), bf(rw_a2[j]), vres,
                bf(rw_g1[j]), bf(rw_g2[j]))
            if j == 0:
                v_first = v
            y = _wkv(r, lw, k, v, al, rowv(rw_k_k[j]), rowv(rw_k_a[j]), rowv(rw_r_k[j]),
                     rowv(rw_lnx_g[j]), rowv(rw_lnx_b[j]), batch, seq_len)
            x2 = _rwkv_out(x2, y, g, bf(rw_w_out[j]), lg, lb, alpha)
        else:
            n_groups, chunk = sg_b_s.shape[1], sg_b_s.shape[2]
            bias = jnp.broadcast_to(sg_b_s[j][:, :, None], (n_groups, chunk, LANES))
            x2 = _spatial_gating(
                x2, bf(sg_w_in[j]), rowv(sg_b_in[j]), rowv(sg_ln_g[j]), rowv(sg_ln_b[j]),
                bf(sg_w_s[j]), bias, bf(sg_w_out[j]), lg, lb, alpha)
        x2 = _ffn(x2, bf(ff_w_gate_up[i]), bf(ff_w_down[i]),
                  rowv(ln_g[i, 1]), rowv(ln_b[i, 1]), alpha)
    return x2.reshape(batch, seq_len, d)
```

```python
import functools

import jax
import jax.numpy as jnp
from jax import lax
from jax.experimental import pallas as pl
from jax.experimental.pallas import tpu as pltpu

F32 = jnp.float32
BF16 = jnp.bfloat16

RW_HEAD_DIM = 64
LNX_EPS = 64e-5
LN_EPS = 1e-5
SG_CHUNK = 128

LANES = 128
MXU_COLS = 256
VMEM_LIMIT_BYTES = 56 * 1024 * 1024

WKV_CHUNK = 64
GROUP_LANES = MXU_COLS


def _dot(a, b):
    return jnp.dot(a, b, preferred_element_type=F32)


def _dot_nt(a, b):
    return lax.dot_general(a, b, (((1,), (1,)), ((), ())), preferred_element_type=F32)


def _dot_tn(a, b):
    return lax.dot_general(a, b, (((0,), (0,)), ((), ())), preferred_element_type=F32)


def _sigmoid(x):
    return 1.0 / (1.0 + jnp.exp(-x))


def _layer_norm(y, g, b):
    mean = jnp.mean(y, axis=-1, keepdims=True)
    d = y - mean
    var = jnp.mean(d * d, axis=-1, keepdims=True)
    return d * lax.rsqrt(var + LN_EPS) * g + b


def _split3(x):
    hi = x.astype(BF16)
    r1 = x - hi.astype(F32)
    mid = r1.astype(BF16)
    lo = (r1 - mid.astype(F32)).astype(BF16)
    return hi, mid, lo


def _const_spec(shape):
    nd = len(shape)
    return pl.BlockSpec(shape, lambda *_: (0,) * nd)


def _params(n_parallel):
    return pltpu.CompilerParams(
        dimension_semantics=("arbitrary",) * n_parallel,
        vmem_limit_bytes=VMEM_LIMIT_BYTES,
    )


def _rwkv_pre_kernel(has_vres, tiles_per_seq, *refs):
    if has_vres:
        (x_ref, xp_ref, vf_ref, mu_ref, wrkv_ref, w0_ref, w1_ref, w2_ref, a0_ref, a1_ref,
         a2_ref, v0_ref, v1_ref, v2_ref, g1_ref, g2_ref,
         r_out, lw_out, k_out, v_out, al_out, g_out) = refs
    else:
        (x_ref, xp_ref, mu_ref, wrkv_ref, w0_ref, w1_ref, w2_ref, a0_ref, a1_ref,
         a2_ref, g1_ref, g2_ref,
         r_out, lw_out, k_out, v_out, al_out, g_out) = refs
    i = pl.program_id(0)
    x = x_ref[...]
    prev_row = xp_ref[7:8, :] * jnp.where(i % tiles_per_seq == 0, 0.0, 1.0)
    row = lax.broadcasted_iota(jnp.int32, x.shape, 0)
    shifted = jnp.where(row == 0, prev_row, pltpu.roll(x, 1, 0))
    xx = shifted - x

    def mix(j):
        return (x + xx * mu_ref[j:j + 1, :]).astype(BF16)

    r_out[...] = _dot(mix(0), wrkv_ref[0])
    wl = w0_ref[...] + _dot(jnp.tanh(_dot(mix(1), w1_ref[...])).astype(BF16), w2_ref[...])
    nz = -wl
    softplus = jnp.maximum(nz, 0.0) + jnp.log(1.0 + jnp.exp(-jnp.abs(nz)))
    lw_out[...] = -jnp.exp(-softplus - 0.5)
    k_out[...] = _dot(mix(2), wrkv_ref[1])
    xv = mix(3)
    v = _dot(xv, wrkv_ref[2])
    if has_vres:
        gate = _sigmoid(v0_ref[...] + _dot(_dot(xv, v1_ref[...]).astype(BF16), v2_ref[...]))
        v = v + (vf_ref[...] - v) * gate
    v_out[...] = v
    al_out[...] = _sigmoid(
        a0_ref[...] + _dot(_dot(mix(4), a1_ref[...]).astype(BF16), a2_ref[...]))
    g_out[...] = _dot(_sigmoid(_dot(mix(5), g1_ref[...])).astype(BF16), g2_ref[...])


def _rwkv_pre(x2, v_first, seq_len, mu, wrkv, w0, w1, w2, a0, a1, a2, vres, g1, g2, tm=256):
    n, d = x2.shape
    has_vres = vres is not None
    row_spec = pl.BlockSpec((tm, d), lambda i: (i, 0))
    prev_spec = pl.BlockSpec((8, d), lambda i: (jnp.maximum(i * (tm // 8) - 1, 0), 0))
    ops = [x2, x2]
    specs = [row_spec, prev_spec]
    if has_vres:
        ops.append(v_first)
        specs.append(row_spec)
    consts = [mu, wrkv, w0, w1, w2, a0, a1, a2]
    if has_vres:
        consts += list(vres)
    consts += [g1, g2]
    ops += consts
    specs += [_const_spec(c.shape) for c in consts]
    out = jax.ShapeDtypeStruct((n, d), F32)
    return pl.pallas_call(
        functools.partial(_rwkv_pre_kernel, has_vres, seq_len // tm),
        grid=(n // tm,),
        in_specs=specs,
        out_specs=[row_spec] * 6,
        out_shape=[out] * 6,
        compiler_params=_params(1),
        name="rwkv_pre",
    )(*ops)


def _wkv_kernel(n_sub, r_ref, lw_ref, k_ref, v_ref, al_ref, kk_ref, ka_ref, rk_ref, lg_ref,
                lb_ref, o_ref, st_ref):
    c = WKV_CHUNK
    gl = GROUP_LANES
    hd = RW_HEAD_DIM
    n_groups = r_ref.shape[1] // gl
    units = [(s, g) for s in range(n_sub) for g in range(n_groups)]

    @pl.when(pl.program_id(1) == 0)
    def _():
        st_ref[...] = jnp.zeros_like(st_ref)

    row = lax.broadcasted_iota(jnp.int32, (gl, gl), 0)
    col = lax.broadcasted_iota(jnp.int32, (gl, gl), 1)
    same_head = (row // hd) == (col // hd)
    head_sum = jnp.where(same_head, 1.0, 0.0).astype(BF16)
    t_row = lax.broadcasted_iota(jnp.int32, (c, gl), 0)
    t_col = lax.broadcasted_iota(jnp.int32, (c, gl), 1) & (hd - 1)
    tri_strict = t_row > t_col
    tri_incl = t_row >= t_col
    eye = jnp.where(t_row == t_col, 1.0, 0.0).astype(F32)
    crow = lax.broadcasted_iota(jnp.int32, (c, c), 0)
    ccol = lax.broadcasted_iota(jnp.int32, (c, c), 1)
    cum_mat = jnp.where(crow >= ccol, 1.0, 0.0).astype(BF16)
    lane = lax.broadcasted_iota(jnp.int32, (c, LANES), 1)
    half = [jnp.where(lane < hd, 1.0, 0.0).astype(BF16),
            jnp.where(lane >= hd, 1.0, 0.0).astype(BF16)]
    zero_slab = jnp.zeros((c, LANES), BF16)

    def block_diag(z):
        zb = z.astype(BF16)
        blocks = []
        for h in range(gl // hd):
            j = (h * hd) // LANES
            slab = zb[:, j * LANES:(j + 1) * LANES] * half[h % (LANES // hd)]
            pieces = [zero_slab] * (gl // LANES)
            pieces[j] = slab
            blocks.append(jnp.concatenate(pieces, axis=1))
        return jnp.concatenate(blocks, axis=0)

    def seg_sum(z):
        return _dot(z.astype(BF16), head_sum)

    ctx = {}
    for (s, g) in units:
        rows = slice(s * c, (s + 1) * c)
        sl = slice(g * gl, (g + 1) * gl)
        r = r_ref[rows, sl]
        lw = lw_ref[rows, sl]
        k = k_ref[rows, sl]
        v = v_ref[rows, sl]
        al = al_ref[rows, sl]
        kk = k * kk_ref[:, sl]
        k2 = k * (1.0 + (al - 1.0) * ka_ref[:, sl])
        hi, mid, lo = _split3(lw)
        cum = _dot(cum_mat, hi) + _dot(cum_mat, mid) + _dot(cum_mat, lo)
        ctx[s, g] = dict(r=r, lw=lw, v=v, al=al, kk=kk, k2=k2, cum=cum)
    for u_ in units:
        d = ctx[u_]
        d["ss"] = seg_sum(d["kk"] * d["kk"])
    for (s, g) in units:
        d = ctx[s, g]
        sl = slice(g * gl, (g + 1) * gl)
        kk = d["kk"] / jnp.maximum(jnp.sqrt(d["ss"]), 1e-12)
        b = kk * d["al"]
        cum, lw = d["cum"], d["lw"]
        w_incl = jnp.exp(cum)
        w_excl = jnp.exp(cum - lw)
        w_inv = jnp.exp(-cum)
        w_end = w_incl[c - 1:c, :]
        w_tail = w_inv * w_end
        d["w_end"] = w_end
        d["ar"] = jnp.concatenate(
            [(-kk * w_excl).astype(BF16), (d["r"] * w_incl).astype(BF16)], axis=0)
        d["bk_tail"] = jnp.concatenate(
            [(b * w_tail).astype(BF16), (d["k2"] * w_tail).astype(BF16)], axis=0)
        d["bk_bd"] = jnp.concatenate(
            [block_diag(b * w_inv), block_diag(d["k2"] * w_inv)], axis=0)
        d["v_bd"] = block_diag(d["v"])
        d["bonus"] = d["r"] * d["k2"] * rk_ref[:, sl]
    for u_ in units:
        d = ctx[u_]
        sc = _dot_nt(d["ar"], d["bk_bd"])
        d["n_ab"] = jnp.where(tri_strict, sc[:c, :gl], 0.0)
        d["n_ak"] = jnp.where(tri_strict, sc[:c, gl:], 0.0).astype(BF16)
        d["n_rb"] = jnp.where(tri_incl, sc[c:, :gl], 0.0).astype(BF16)
        d["n_rk"] = jnp.where(tri_incl, sc[c:, gl:], 0.0).astype(BF16)
        d["bonus"] = seg_sum(d["bonus"]) * d["v"]

    for u_ in units:
        d = ctx[u_]
        d["inv"] = eye + d["n_ab"]
        d["pw"] = d["n_ab"]
    for _ in range(c.bit_length() - 2):
        for u_ in units:
            d = ctx[u_]
            d["pw"] = _dot(d["pw"].astype(BF16), block_diag(d["pw"]))
        for u_ in units:
            d = ctx[u_]
            d["inv"] = d["inv"] + _dot(d["inv"].astype(BF16), block_diag(d["pw"]))

    for s in range(n_sub):
        gs = [ctx[s, g] for g in range(n_groups)]
        for g, d in enumerate(gs):
            d["state"] = st_ref[g]
            d["fs"] = _dot_nt(d["ar"], d["state"].astype(BF16))
        for d in gs:
            d["rhs"] = d["fs"][:c] + _dot(d["n_ak"], d["v_bd"])
        for d in gs:
            d["u"] = _dot(d["inv"].astype(BF16), block_diag(d["rhs"]))
        for g, d in enumerate(gs):
            u = d["u"]
            d["y"] = d["fs"][c:] + _dot(d["n_rb"], block_diag(u)) + _dot(d["n_rk"], d["v_bd"])
            uv = jnp.concatenate([u.astype(BF16), d["v"].astype(BF16)], axis=0)
            upd = d["state"] * d["w_end"] + _dot_tn(uv, d["bk_tail"])
            st_ref[g] = jnp.where(same_head, upd, 0.0)

    inv_n = 1.0 / hd
    for u_ in units:
        d = ctx[u_]
        d["dlt"] = d["y"] - seg_sum(d["y"]) * inv_n
    for u_ in units:
        d = ctx[u_]
        d["var"] = seg_sum(d["dlt"] * d["dlt"]) * inv_n
    for (s, g) in units:
        d = ctx[s, g]
        rows = slice(s * c, (s + 1) * c)
        sl = slice(g * gl, (g + 1) * gl)
        yn = d["dlt"] * lax.rsqrt(d["var"] + LNX_EPS) * lg_ref[:, sl] + lb_ref[:, sl]
        o_ref[rows, sl] = yn + d["bonus"]


def _wkv(r, lw, k, v, al, k_k, k_a, r_k, lnx_g, lnx_b, batch, seq_len, n_sub=1):
    n, d = r.shape
    rows = WKV_CHUNK * n_sub
    nc = seq_len // rows
    row_spec = pl.BlockSpec((rows, d), lambda bi, ci: (bi * nc + ci, 0))
    vec_spec = pl.BlockSpec((1, d), lambda bi, ci: (0, 0))
    return pl.pallas_call(
        functools.partial(_wkv_kernel, n_sub),
        grid=(batch, nc),
        in_specs=[row_spec] * 5 + [vec_spec] * 5,
        out_specs=row_spec,
        out_shape=jax.ShapeDtypeStruct((n, d), F32),
        scratch_shapes=[pltpu.VMEM((d // GROUP_LANES, GROUP_LANES, GROUP_LANES), F32)],
        compiler_params=_params(2),
        name="wkv",
    )(r, lw, k, v, al, k_k, k_a, r_k, lnx_g, lnx_b)


def _rwkv_out_kernel(alpha, x_ref, y_ref, g_ref, w_ref, lg_ref, lb_ref, o_ref):
    h = _dot((y_ref[...] * g_ref[...]).astype(BF16), w_ref[...])
    o_ref[...] = _layer_norm(alpha * x_ref[...] + h, lg_ref[...], lb_ref[...])


def _rwkv_out(x2, y, g, w_out, ln_g, ln_b, alpha, tm=512):
    n, d = x2.shape
    row_spec = pl.BlockSpec((tm, d), lambda i: (i, 0))
    return pl.pallas_call(
        functools.partial(_rwkv_out_kernel, alpha),
        grid=(n // tm,),
        in_specs=[row_spec] * 3 + [_const_spec(w_out.shape), _const_spec(ln_g.shape),
                                   _const_spec(ln_b.shape)],
        out_specs=row_spec,
        out_shape=jax.ShapeDtypeStruct((n, d), F32),
        compiler_params=_params(1),
        name="rwkv_out",
    )(x2, y, g, w_out, ln_g, ln_b)


def _sg_kernel(alpha, x_ref, win_ref, bin_ref, lng_ref, lnb_ref, ws_ref, bs_ref, wout_ref,
               lg_ref, lb_ref, o_ref, gated_ref):
    x = x_ref[...]
    tm = x.shape[0]
    e = wout_ref.shape[0]
    xb = x.astype(BF16)

    def gelu(z):
        return 0.5 * z * (1.0 + jnp.tanh(0.7978845608028654 * (z + 0.044715 * z * z * z)))

    v = gelu(_dot(xb, win_ref[:, e:]) + bin_ref[:, e:])
    v = _layer_norm(v, lng_ref[...], lnb_ref[...]).astype(BF16)
    n_groups = ws_ref.shape[0]
    gd = e // n_groups
    row = lax.broadcasted_iota(jnp.int32, (SG_CHUNK, SG_CHUNK), 0)
    col = lax.broadcasted_iota(jnp.int32, (SG_CHUNK, SG_CHUNK), 1)
    causal = row >= col
    groups_per_pass = max(1, MXU_COLS // gd)
    for g0 in range(0, n_groups, groups_per_pass):
        wide = slice(g0 * gd, (g0 + groups_per_pass) * gd)
        u = gelu(_dot(xb, win_ref[:, wide]) + bin_ref[:, wide])
        for gi in range(groups_per_pass):
            g = g0 + gi
            ws = jnp.where(causal, ws_ref[g], jnp.zeros((), BF16))
            cols = slice(g * gd, (g + 1) * gd)
            for n in range(tm // SG_CHUNK):
                rows = slice(n * SG_CHUNK, (n + 1) * SG_CHUNK)
                mixed = _dot(ws, v[rows, cols]) + bs_ref[g]
                gated_ref[rows, cols] = (u[rows, gi * gd:(gi + 1) * gd] * mixed).astype(BF16)
    h = _dot(gated_ref[...], wout_ref[...])
    o_ref[...] = _layer_norm(alpha * x + h, lg_ref[...], lb_ref[...])


def _spatial_gating(x2, w_in, b_in, ln_g, ln_b, w_s, b_s, w_out, lg, lb, alpha, tm=256):
    n, d = x2.shape
    e = w_out.shape[0]
    row_spec = pl.BlockSpec((tm, d), lambda i: (i, 0))
    consts = [w_in, b_in, ln_g, ln_b, w_s, b_s, w_out, lg, lb]
    return pl.pallas_call(
        functools.partial(_sg_kernel, alpha),
        grid=(n // tm,),
        in_specs=[row_spec] + [_const_spec(c.shape) for c in consts],
        out_specs=row_spec,
        out_shape=jax.ShapeDtypeStruct((n, d), F32),
        scratch_shapes=[pltpu.VMEM((tm, e), BF16)],
        compiler_params=_params(1),
        name="spatial_gating",
    )(x2, *consts)


def _ffn_kernel(alpha, f_chunk, x_ref, wgu_ref, wd_ref, lg_ref, lb_ref, o_ref, acc_ref):
    x = x_ref[...]
    xb = x.astype(BF16)
    f = wd_ref.shape[0]
    for ci in range(f // f_chunk):
        lo = ci * f_chunk
        gate = _dot(xb, wgu_ref[:, lo:lo + f_chunk])
        up = _dot(xb, wgu_ref[:, f + lo:f + lo + f_chunk])
        act = (gate * _sigmoid(gate) * up).astype(BF16)
        part = _dot(act, wd_ref[lo:lo + f_chunk, :])
        if ci == 0:
            acc_ref[...] = part
        else:
            acc_ref[...] += part
    o_ref[...] = _layer_norm(alpha * x + acc_ref[...], lg_ref[...], lb_ref[...])


def _ffn(x2, w_gate_up, w_down, lg, lb, alpha, tm=512, f_chunk=256):
    n, d = x2.shape
    row_spec = pl.BlockSpec((tm, d), lambda i: (i, 0))
    consts = [w_gate_up, w_down, lg, lb]
    return pl.pallas_call(
        functools.partial(_ffn_kernel, alpha, f_chunk),
        grid=(n // tm,),
        in_specs=[row_spec] + [_const_spec(c.shape) for c in consts],
        out_specs=row_spec,
        out_shape=jax.ShapeDtypeStruct((n, d), F32),
        scratch_shapes=[pltpu.VMEM((tm, d), F32)],
        compiler_params=_params(1),
        name="swiglu",
    )(x2, *consts)


def kernel(x, rw_mu, rw_w_rkv, rw_w0, rw_w1, rw_w2, rw_a0, rw_a1, rw_a2, rw_v0, rw_v1, rw_v2, rw_g1, rw_g2, rw_k_k, rw_k_a, rw_r_k, rw_lnx_g, rw_lnx_b, rw_w_out, sg_w_in, sg_b_in, sg_ln_g, sg_ln_b, sg_w_s, sg_b_s, sg_w_out, ff_w_gate_up, ff_w_down, ln_g, ln_b):
    batch, seq_len, d = x.shape
    depth = ff_w_gate_up.shape[0]
    alpha = (2.0 * depth) ** 0.25
    n = batch * seq_len
    x2 = x.reshape(n, d)
    bf = lambda w: w.astype(BF16)
    rowv = lambda p: p.reshape(1, -1)

    v_first = None
    for i in range(depth):
        j = i // 2
        lg, lb = rowv(ln_g[i, 0]), rowv(ln_b[i, 0])
        if i % 2 == 0:
            vres = None if j == 0 else (rowv(rw_v0[j - 1]), bf(rw_v1[j - 1]), bf(rw_v2[j - 1]))
            r, lw, k, v, al, g = _rwkv_pre(
                x2, v_first, seq_len, rw_mu[j], bf(rw_w_rkv[j]), rowv(rw_w0[j]), bf(rw_w1[j]),
                bf(rw_w2[j]), rowv(rw_a0[j]), bf(rw_a1[j]), bf(rw_a2[j]), vres,
                bf(rw_g1[j]), bf(rw_g2[j]))
            if j == 0:
                v_first = v
            y = _wkv(r, lw, k, v, al, rowv(rw_k_k[j]), rowv(rw_k_a[j]), rowv(rw_r_k[j]),
                     rowv(rw_lnx_g[j]), rowv(rw_lnx_b[j]), batch, seq_len)
            x2 = _rwkv_out(x2, y, g, bf(rw_w_out[j]), lg, lb, alpha)
        else:
            n_groups, chunk = sg_b_s.shape[1], sg_b_s.shape[2]
            bias = jnp.broadcast_to(sg_b_s[j][:, :, None], (n_groups, chunk, LANES))
            x2 = _spatial_gating(
                x2, bf(sg_w_in[j]), rowv(sg_b_in[j]), rowv(sg_ln_g[j]), rowv(sg_ln_b[j]),
                bf(sg_w_s[j]), bias, bf(sg_w_out[j]), lg, lb, alpha)
        x2 = _ffn(x2, bf(ff_w_gate_up[i]), bf(ff_w_down[i]),
                  rowv(ln_g[i, 1]), rowv(ln_b[i, 1]), alpha)
    return x2.reshape(batch, seq_len, d)
```

```python
import functools

import jax
import jax.numpy as jnp
from jax import lax
from jax.experimental import pallas as pl
from jax.experimental.pallas import tpu as pltpu

F32 = jnp.float32
BF16 = jnp.bfloat16

RW_HEAD_DIM = 64
LNX_EPS = 64e-5
LN_EPS = 1e-5
SG_CHUNK = 128

LANES = 128
MXU_COLS = 256
VMEM_LIMIT_BYTES = 56 * 1024 * 1024

WKV_CHUNK = 64
GROUP_LANES = MXU_COLS
assert WKV_CHUNK == RW_HEAD_DIM
WKV_SKEW = 3


def _dot(a, b):
    return jnp.dot(a, b, preferred_element_type=F32)


def _dot_nt(a, b):
    return lax.dot_general(a, b, (((1,), (1,)), ((), ())), preferred_element_type=F32)


def _dot_tn(a, b):
    return lax.dot_general(a, b, (((0,), (0,)), ((), ())), preferred_element_type=F32)


def _sigmoid(x):
    return 1.0 / (1.0 + jnp.exp(-x))


def _layer_norm(y, g, b):
    mean = jnp.mean(y, axis=-1, keepdims=True)
    d = y - mean
    var = jnp.mean(d * d, axis=-1, keepdims=True)
    return d * lax.rsqrt(var + LN_EPS) * g + b


def _split3(x):
    hi = x.astype(BF16)
    r1 = x - hi.astype(F32)
    mid = r1.astype(BF16)
    lo = (r1 - mid.astype(F32)).astype(BF16)
    return hi, mid, lo


def _const_spec(shape):
    nd = len(shape)
    return pl.BlockSpec(shape, lambda *_: (0,) * nd, pipeline_mode=pl.Buffered(1))


def _params(n_parallel):
    return pltpu.CompilerParams(
        dimension_semantics=("arbitrary",) * n_parallel,
        vmem_limit_bytes=VMEM_LIMIT_BYTES,
    )


def _rwkv_pre_kernel(has_vres, tiles_per_seq, *refs):
    if has_vres:
        (x_ref, xp_ref, vf_ref, mu_ref, wrkv_ref, w0_ref, w1_ref, w2_ref, a0_ref, a1_ref,
         a2_ref, v0_ref, v1_ref, v2_ref, g1_ref, g2_ref,
         r_out, lw_out, k_out, v_out, al_out, g_out) = refs
    else:
        (x_ref, xp_ref, mu_ref, wrkv_ref, w0_ref, w1_ref, w2_ref, a0_ref, a1_ref,
         a2_ref, g1_ref, g2_ref,
         r_out, lw_out, k_out, v_out, al_out, g_out) = refs
    i = pl.program_id(0)
    x = x_ref[...]
    prev_row = xp_ref[7:8, :] * jnp.where(i % tiles_per_seq == 0, 0.0, 1.0)
    row = lax.broadcasted_iota(jnp.int32, x.shape, 0)
    shifted = jnp.where(row == 0, prev_row, pltpu.roll(x, 1, 0))
    xx = shifted - x

    def mix(j):
        return (x + xx * mu_ref[j:j + 1, :]).astype(BF16)

    r_out[...] = _dot(mix(0), wrkv_ref[0])
    wl = w0_ref[...] + _dot(jnp.tanh(_dot(mix(1), w1_ref[...])).astype(BF16), w2_ref[...])
    nz = -wl
    softplus = jnp.maximum(nz, 0.0) + jnp.log(1.0 + jnp.exp(-jnp.abs(nz)))
    lw_out[...] = -jnp.exp(-softplus - 0.5)
    k_out[...] = _dot(mix(2), wrkv_ref[1])
    xv = mix(3)
    v = _dot(xv, wrkv_ref[2])
    if has_vres:
        gate = _sigmoid(v0_ref[...] + _dot(_dot(xv, v1_ref[...]).astype(BF16), v2_ref[...]))
        v = v + (vf_ref[...] - v) * gate
    v_out[...] = v
    al_out[...] = _sigmoid(
        a0_ref[...] + _dot(_dot(mix(4), a1_ref[...]).astype(BF16), a2_ref[...]))
    g_out[...] = _dot(_sigmoid(_dot(mix(5), g1_ref[...])).astype(BF16), g2_ref[...])


def _rwkv_pre(x2, v_first, seq_len, mu, wrkv, w0, w1, w2, a0, a1, a2, vres, g1, g2, tm=256):
    n, d = x2.shape
    has_vres = vres is not None
    row_spec = pl.BlockSpec((tm, d), lambda i: (i, 0))
    prev_spec = pl.BlockSpec((8, d), lambda i: (jnp.maximum(i * (tm // 8) - 1, 0), 0))
    ops = [x2, x2]
    specs = [row_spec, prev_spec]
    if has_vres:
        ops.append(v_first)
        specs.append(row_spec)
    consts = [mu, wrkv, w0, w1, w2, a0, a1, a2]
    if has_vres:
        consts += list(vres)
    consts += [g1, g2]
    ops += consts
    specs += [_const_spec(c.shape) for c in consts]
    out = jax.ShapeDtypeStruct((n, d), F32)
    return pl.pallas_call(
        functools.partial(_rwkv_pre_kernel, has_vres, seq_len // tm),
        grid=(n // tm,),
        in_specs=specs,
        out_specs=[row_spec] * 6,
        out_shape=[out] * 6,
        compiler_params=_params(1),
        name="rwkv_pre",
    )(*ops)


def _wkv_kernel(n_sub, r_ref, lw_ref, k_ref, v_ref, al_ref, kk_ref, ka_ref, rk_ref, lg_ref,
                lb_ref, o_ref, st_ref):
    c = WKV_CHUNK
    gl = GROUP_LANES
    hd = RW_HEAD_DIM
    n_groups = r_ref.shape[1] // gl
    units = [(s, g) for s in range(n_sub) for g in range(n_groups)]

    @pl.when(pl.program_id(1) == 0)
    def _():
        st_ref[...] = jnp.zeros_like(st_ref)

    row = lax.broadcasted_iota(jnp.int32, (gl, gl), 0)
    col = lax.broadcasted_iota(jnp.int32, (gl, gl), 1)
    same_head = (row // hd) == (col // hd)
    head_sum = jnp.where(same_head, 1.0, 0.0).astype(BF16)
    t_row = lax.broadcasted_iota(jnp.int32, (c, gl), 0)
    t_col = lax.broadcasted_iota(jnp.int32, (c, gl), 1) & (hd - 1)
    tri_strict = t_row > t_col
    tri_incl = t_row >= t_col
    eye = jnp.where(t_row == t_col, 1.0, 0.0).astype(F32)
    crow = lax.broadcasted_iota(jnp.int32, (c, c), 0)
    ccol = lax.broadcasted_iota(jnp.int32, (c, c), 1)
    cum_mat = jnp.where(crow >= ccol, 1.0, 0.0).astype(BF16)
    lane = lax.broadcasted_iota(jnp.int32, (c, LANES), 1)
    half = [jnp.where(lane < hd, 1.0, 0.0).astype(BF16),
            jnp.where(lane >= hd, 1.0, 0.0).astype(BF16)]
    zero_slab = jnp.zeros((c, LANES), BF16)

    def block_diag(z):
        zb = z.astype(BF16)
        blocks = []
        for h in range(gl // hd):
            j = (h * hd) // LANES
            slab = zb[:, j * LANES:(j + 1) * LANES] * half[h % (LANES // hd)]
            pieces = [zero_slab] * (gl // LANES)
            pieces[j] = slab
            blocks.append(jnp.concatenate(pieces, axis=1))
        return jnp.concatenate(blocks, axis=0)

    def block_diag_t(z):
        zt = jnp.transpose(jnp.concatenate([z] * (LANES // c), axis=0)).astype(BF16)
        blocks = []
        for h in range(gl // hd):
            slab = zt[h * hd:(h + 1) * hd, :] * half[h % (LANES // hd)]
            pieces = [zero_slab] * (gl // LANES)
            pieces[(h * c) // LANES] = slab
            blocks.append(jnp.concatenate(pieces, axis=1))
        return jnp.concatenate(blocks, axis=0)

    def seg_sums(zs):
        out = _dot(jnp.concatenate([z.astype(BF16) for z in zs], axis=0), head_sum)
        return [out[i * c:(i + 1) * c] for i in range(len(zs))]

    groups = range(n_groups)
    state = [st_ref[g] for g in groups]

    def chunk_steps(s):
        rows = slice(s * c, (s + 1) * c)
        ds = []
        for g in groups:
            sl = slice(g * gl, (g + 1) * gl)
            r = r_ref[rows, sl]
            lw = lw_ref[rows, sl]
            k = k_ref[rows, sl]
            v = v_ref[rows, sl]
            al = al_ref[rows, sl]
            kk = k * kk_ref[:, sl]
            k2 = k * (1.0 + (al - 1.0) * ka_ref[:, sl])
            hi, mid, lo = _split3(lw)
            cum = _dot(cum_mat, hi) + _dot(cum_mat, mid) + _dot(cum_mat, lo)
            ds.append(dict(r=r, lw=lw, v=v, al=al, kk=kk, k2=k2, cum=cum,
                           bonus=r * k2 * rk_ref[:, sl]))
        sums = seg_sums([d["kk"] * d["kk"] for d in ds] + [d["bonus"] for d in ds])
        for g, d in enumerate(ds):
            d["ss"] = sums[g]
            d["bonus"] = sums[n_groups + g] * d["v"]
        yield
        for d in ds:
            kk = d["kk"] / jnp.maximum(jnp.sqrt(d["ss"]), 1e-12)
            b = kk * d["al"]
            cum, lw = d["cum"], d["lw"]
            w_incl = jnp.exp(cum)
            w_excl = jnp.exp(cum - lw)
            w_inv = jnp.exp(-cum)
            w_end = w_incl[c - 1:c, :]
            w_tail = w_inv * w_end
            d["w_end"] = w_end
            d["ar"] = jnp.concatenate(
                [(-kk * w_excl).astype(BF16), (d["r"] * w_incl).astype(BF16)], axis=0)
            d["bk_tail"] = jnp.concatenate(
                [(b * w_tail).astype(BF16), (d["k2"] * w_tail).astype(BF16)], axis=0)
            d["bk_bd"] = jnp.concatenate(
                [block_diag_t(b * w_inv), block_diag_t(d["k2"] * w_inv)], axis=1)
            d["v_bd"] = block_diag(d["v"])
        yield
        for d in ds:
            sc = _dot(d["ar"], d["bk_bd"])
            d["pw"] = jnp.where(tri_strict, sc[:c, :gl], 0.0)
            d["inv"] = eye + d["pw"]
            n_ak = jnp.where(tri_strict, sc[:c, gl:], 0.0).astype(BF16)
            d["n_rb"] = jnp.where(tri_incl, sc[c:, :gl], 0.0).astype(BF16)
            n_rk = jnp.where(tri_incl, sc[c:, gl:], 0.0).astype(BF16)
            kv = _dot(jnp.concatenate([n_ak, n_rk], axis=0), d["v_bd"])
            d["akv"] = kv[:c]
            d["rkv"] = kv[c:]
        yield
        levels = c.bit_length() - 1
        for l in range(levels):
            first, last = l == 0, l == levels - 1
            for d in ds:
                pwb = d["pw"].astype(BF16)
                lhs = ([] if first else [d["inv"].astype(BF16)]) + ([] if last else [pwb])
                out = _dot(jnp.concatenate(lhs, axis=0), block_diag(pwb))
                if not first:
                    d["inv"] = d["inv"] + out[:c]
                if not last:
                    d["pw"] = out[out.shape[0] - c:]
            yield
        assert chunks_done[0] == s
        for g, d in enumerate(ds):
            d["fs"] = _dot_nt(d["ar"], state[g].astype(BF16))
        yield
        for d in ds:
            d["u"] = _dot(d["inv"].astype(BF16), block_diag(d["fs"][:c] + d["akv"]))
        yield
        for g, d in enumerate(ds):
            u = d["u"]
            d["y"] = d["fs"][c:] + _dot(d["n_rb"], block_diag(u)) + d["rkv"]
            uv = jnp.concatenate([u.astype(BF16), d["v"].astype(BF16)], axis=0)
            upd = state[g] * d["w_end"] + _dot_tn(uv, d["bk_tail"])
            state[g] = jnp.where(same_head, upd, 0.0)
        chunks_done[0] = s + 1
        yield
        inv_n = 1.0 / hd
        means = seg_sums([d["y"] for d in ds])
        dlts = [d["y"] - m * inv_n for d, m in zip(ds, means)]
        yield
        variances = seg_sums([dl * dl for dl in dlts])
        for g, (d, dl, var) in enumerate(zip(ds, dlts, variances)):
            sl = slice(g * gl, (g + 1) * gl)
            yn = dl * lax.rsqrt(var * inv_n + LNX_EPS) * lg_ref[:, sl] + lb_ref[:, sl]
            o_ref[rows, sl] = yn + d["bonus"]

    chunks_done = [0]
    pending = [(s * WKV_SKEW, chunk_steps(s)) for s in range(n_sub)]
    tick = 0
    while pending:
        pending = [(t0, gen) for t0, gen in pending
                   if tick < t0 or next(gen, "done") != "done"]
        tick += 1
    for g in groups:
        st_ref[g] = state[g]


def _wkv(r, lw, k, v, al, k_k, k_a, r_k, lnx_g, lnx_b, batch, seq_len, n_sub=4):
    n, d = r.shape
    rows = WKV_CHUNK * n_sub
    nc = seq_len // rows
    row_spec = pl.BlockSpec((rows, d), lambda bi, ci: (bi * nc + ci, 0))
    vec_spec = pl.BlockSpec((1, d), lambda bi, ci: (0, 0))
    return pl.pallas_call(
        functools.partial(_wkv_kernel, n_sub),
        grid=(batch, nc),
        in_specs=[row_spec] * 5 + [vec_spec] * 5,
        out_specs=row_spec,
        out_shape=jax.ShapeDtypeStruct((n, d), F32),
        scratch_shapes=[pltpu.VMEM((d // GROUP_LANES, GROUP_LANES, GROUP_LANES), F32)],
        compiler_params=_params(2),
        name="wkv",
    )(r, lw, k, v, al, k_k, k_a, r_k, lnx_g, lnx_b)


def _sg_kernel(alpha, x_ref, win_ref, bin_ref, lng_ref, lnb_ref, ws_ref, bs_ref, wout_ref,
               lg_ref, lb_ref, o_ref, gated_ref):
    x = x_ref[...]
    tm = x.shape[0]
    e = wout_ref.shape[0]
    xb = x.astype(BF16)

    def gelu(z):
        return 0.5 * z * (1.0 + jnp.tanh(0.7978845608028654 * (z + 0.044715 * z * z * z)))

    v = gelu(_dot(xb, win_ref[:, e:]) + bin_ref[:, e:])
    v = _layer_norm(v, lng_ref[...], lnb_ref[...]).astype(BF16)
    n_groups = ws_ref.shape[0]
    gd = e // n_groups
    row = lax.broadcasted_iota(jnp.int32, (SG_CHUNK, SG_CHUNK), 0)
    col = lax.broadcasted_iota(jnp.int32, (SG_CHUNK, SG_CHUNK), 1)
    causal = row >= col
    groups_per_pass = max(1, MXU_COLS // gd)
    for g0 in range(0, n_groups, groups_per_pass):
        wide = slice(g0 * gd, (g0 + groups_per_pass) * gd)
        u = gelu(_dot(xb, win_ref[:, wide]) + bin_ref[:, wide])
        for gi in range(groups_per_pass):
            g = g0 + gi
            ws = jnp.where(causal, ws_ref[g], jnp.zeros((), BF16))
            cols = slice(g * gd, (g + 1) * gd)
            for n in range(tm // SG_CHUNK):
                rows = slice(n * SG_CHUNK, (n + 1) * SG_CHUNK)
                mixed = _dot(ws, v[rows, cols]) + bs_ref[g]
                gated_ref[rows, cols] = (u[rows, gi * gd:(gi + 1) * gd] * mixed).astype(BF16)
    h = _dot(gated_ref[...], wout_ref[...])
    o_ref[...] = _layer_norm(alpha * x + h, lg_ref[...], lb_ref[...])


def _spatial_gating(x2, w_in, b_in, ln_g, ln_b, w_s, b_s, w_out, lg, lb, alpha, tm=256):
    n, d = x2.shape
    e = w_out.shape[0]
    row_spec = pl.BlockSpec((tm, d), lambda i: (i, 0))
    consts = [w_in, b_in, ln_g, ln_b, w_s, b_s, w_out, lg, lb]
    return pl.pallas_call(
        functools.partial(_sg_kernel, alpha),
        grid=(n // tm,),
        in_specs=[row_spec] + [_const_spec(c.shape) for c in consts],
        out_specs=row_spec,
        out_shape=jax.ShapeDtypeStruct((n, d), F32),
        scratch_shapes=[pltpu.VMEM((tm, e), BF16)],
        compiler_params=_params(1),
        name="spatial_gating",
    )(x2, *consts)


def _ffn_kernel(alpha, f_chunk, has_mixer_out, *refs):
    if has_mixer_out:
        (x_ref, y_ref, g_ref, wo_ref, lg1_ref, lb1_ref,
         wgu_ref, wd_ref, lg_ref, lb_ref, o_ref, acc_ref) = refs
        h = _dot((y_ref[...] * g_ref[...]).astype(BF16), wo_ref[...])
        x = _layer_norm(alpha * x_ref[...] + h, lg1_ref[...], lb1_ref[...])
    else:
        x_ref, wgu_ref, wd_ref, lg_ref, lb_ref, o_ref, acc_ref = refs
        x = x_ref[...]
    xb = x.astype(BF16)
    f = wd_ref.shape[0]
    for ci in range(f // f_chunk):
        lo = ci * f_chunk
        gate = _dot(xb, wgu_ref[:, lo:lo + f_chunk])
        up = _dot(xb, wgu_ref[:, f + lo:f + lo + f_chunk])
        act = (gate * _sigmoid(gate) * up).astype(BF16)
        part = _dot(act, wd_ref[lo:lo + f_chunk, :])
        if ci == 0:
            acc_ref[...] = part
        else:
            acc_ref[...] += part
    o_ref[...] = _layer_norm(alpha * x + acc_ref[...], lg_ref[...], lb_ref[...])


def _ffn(x2, mixer_out, w_gate_up, w_down, lg, lb, alpha, tm=512, f_chunk=256):
    n, d = x2.shape
    row_spec = pl.BlockSpec((tm, d), lambda i: (i, 0))
    rows = [x2]
    consts = [w_gate_up, w_down, lg, lb]
    if mixer_out is not None:
        rows += list(mixer_out[:2])
        consts = list(mixer_out[2:]) + consts
    return pl.pallas_call(
        functools.partial(_ffn_kernel, alpha, f_chunk, mixer_out is not None),
        grid=(n // tm,),
        in_specs=[row_spec] * len(rows) + [_const_spec(c.shape) for c in consts],
        out_specs=row_spec,
        out_shape=jax.ShapeDtypeStruct((n, d), F32),
        scratch_shapes=[pltpu.VMEM((tm, d), F32)],
        compiler_params=_params(1),
        name="swiglu",
    )(*rows, *consts)


def kernel(x, rw_mu, rw_w_rkv, rw_w0, rw_w1, rw_w2, rw_a0, rw_a1, rw_a2, rw_v0, rw_v1, rw_v2, rw_g1, rw_g2, rw_k_k, rw_k_a, rw_r_k, rw_lnx_g, rw_lnx_b, rw_w_out, sg_w_in, sg_b_in, sg_ln_g, sg_ln_b, sg_w_s, sg_b_s, sg_w_out, ff_w_gate_up, ff_w_down, ln_g, ln_b):
    batch, seq_len, d = x.shape
    depth = ff_w_gate_up.shape[0]
    alpha = (2.0 * depth) ** 0.25
    n = batch * seq_len
    x2 = x.reshape(n, d)
    bf = lambda w: w.astype(BF16)
    rowv = lambda p: p.reshape(1, -1)

    v_first = None
    for i in range(depth):
        j = i // 2
        lg, lb = rowv(ln_g[i, 0]), rowv(ln_b[i, 0])
        if i % 2 == 0:
            vres = None if j == 0 else (rowv(rw_v0[j - 1]), bf(rw_v1[j - 1]), bf(rw_v2[j - 1]))
            r, lw, k, v, al, g = _rwkv_pre(
                x2, v_first, seq_len, rw_mu[j], bf(rw_w_rkv[j]), rowv(rw_w0[j]), bf(rw_w1[j]),
                bf(rw_w2[j]), rowv(rw_a0[j]), bf(rw_a1[j]), bf(rw_a2[j]), vres,
                bf(rw_g1[j]), bf(rw_g2[j]))
            if j == 0:
                v_first = v
            y = _wkv(r, lw, k, v, al, rowv(rw_k_k[j]), rowv(rw_k_a[j]), rowv(rw_r_k[j]),
                     rowv(rw_lnx_g[j]), rowv(rw_lnx_b[j]), batch, seq_len)
            mixer_out = (y, g, bf(rw_w_out[j]), lg, lb)
        else:
            mixer_out = None
            n_groups, chunk = sg_b_s.shape[1], sg_b_s.shape[2]
            bias = jnp.broadcast_to(sg_b_s[j][:, :, None], (n_groups, chunk, LANES))
            x2 = _spatial_gating(
                x2, bf(sg_w_in[j]), rowv(sg_b_in[j]), rowv(sg_ln_g[j]), rowv(sg_ln_b[j]),
                bf(sg_w_s[j]), bias, bf(sg_w_out[j]), lg, lb, alpha)
        x2 = _ffn(x2, mixer_out, bf(ff_w_gate_up[i]), bf(ff_w_down[i]),
                  rowv(ln_g[i, 1]), rowv(ln_b[i, 1]), alpha)
    return x2.reshape(batch, seq_len, d)
```

```python
import functools

import jax
import jax.numpy as jnp
from jax import lax
from jax.experimental import pallas as pl
from jax.experimental.pallas import tpu as pltpu

F32 = jnp.float32
BF16 = jnp.bfloat16

RW_HEAD_DIM = 64
LNX_EPS = 64e-5
LN_EPS = 1e-5
SG_CHUNK = 128
GELU_C0 = 0.7978845608028654
GELU_C1 = GELU_C0 * 0.044715

LANES = 128
MXU_COLS = 256
VMEM_LIMIT_BYTES = 56 * 1024 * 1024

WKV_CHUNK = 64
GROUP_LANES = MXU_COLS
SG_SUBTILE = 256
SG_V_COLS = 512
SG_SKEW = 8
assert WKV_CHUNK == RW_HEAD_DIM


def _dot(a, b):
    return jnp.dot(a, b, preferred_element_type=F32)


def _dot_nt(a, b):
    return lax.dot_general(a, b, (((1,), (1,)), ((), ())), preferred_element_type=F32)


def _dot_tn(a, b):
    return lax.dot_general(a, b, (((0,), (0,)), ((), ())), preferred_element_type=F32)


def _sigmoid(x):
    return 1.0 / (1.0 + jnp.exp(-x))


def _layer_norm(y, g, b):
    mean = jnp.mean(y, axis=-1, keepdims=True)
    d = y - mean
    var = jnp.mean(d * d, axis=-1, keepdims=True)
    return d * lax.rsqrt(var + LN_EPS) * g + b


def _split3(x):
    hi = x.astype(BF16)
    r1 = x - hi.astype(F32)
    mid = r1.astype(BF16)
    lo = (r1 - mid.astype(F32)).astype(BF16)
    return hi, mid, lo


def _run_pipelined(generators, skew):
    pending = [(i * skew, gen) for i, gen in enumerate(generators)]
    tick = 0
    while pending:
        pending = [(t0, gen) for t0, gen in pending
                   if tick < t0 or next(gen, "done") != "done"]
        tick += 1


def _const_spec(shape):
    nd = len(shape)
    return pl.BlockSpec(shape, lambda *_: (0,) * nd, pipeline_mode=pl.Buffered(1))


def _params(n_parallel):
    return pltpu.CompilerParams(
        dimension_semantics=("arbitrary",) * n_parallel,
        vmem_limit_bytes=VMEM_LIMIT_BYTES,
    )


def _rwkv_pre_kernel(has_vres, tiles_per_seq, *refs):
    if has_vres:
        (x_ref, xp_ref, vf_ref, mu_ref, wrkv_ref, w0_ref, w1_ref, w2_ref, a0_ref, a1_ref,
         a2_ref, v0_ref, v1_ref, v2_ref, g1_ref, g2_ref,
         r_out, lw_out, k_out, v_out, al_out, g_out) = refs
    else:
        (x_ref, xp_ref, mu_ref, wrkv_ref, w0_ref, w1_ref, w2_ref, a0_ref, a1_ref,
         a2_ref, g1_ref, g2_ref,
         r_out, lw_out, k_out, v_out, al_out, g_out) = refs
    i = pl.program_id(0)
    x = x_ref[...]
    prev_row = xp_ref[7:8, :] * jnp.where(i % tiles_per_seq == 0, 0.0, 1.0)
    row = lax.broadcasted_iota(jnp.int32, x.shape, 0)
    shifted = jnp.where(row == 0, prev_row, pltpu.roll(x, 1, 0))
    xx = shifted - x

    def mix(j):
        return (x + xx * mu_ref[j:j + 1, :]).astype(BF16)

    r_out[...] = _dot(mix(0), wrkv_ref[0])
    wl = w0_ref[...] + _dot(jnp.tanh(_dot(mix(1), w1_ref[...])).astype(BF16), w2_ref[...])
    nz = -wl
    softplus = jnp.maximum(nz, 0.0) + jnp.log(1.0 + jnp.exp(-jnp.abs(nz)))
    lw_out[...] = -jnp.exp(-softplus - 0.5)
    k_out[...] = _dot(mix(2), wrkv_ref[1])
    xv = mix(3)
    v = _dot(xv, wrkv_ref[2])
    if has_vres:
        gate = _sigmoid(v0_ref[...] + _dot(_dot(xv, v1_ref[...]).astype(BF16), v2_ref[...]))
        v = v + (vf_ref[...] - v) * gate
    v_out[...] = v
    al_out[...] = _sigmoid(
        a0_ref[...] + _dot(_dot(mix(4), a1_ref[...]).astype(BF16), a2_ref[...]))
    g_out[...] = _dot(_sigmoid(_dot(mix(5), g1_ref[...])).astype(BF16), g2_ref[...])


def _rwkv_pre(x2, v_first, seq_len, mu, wrkv, w0, w1, w2, a0, a1, a2, vres, g1, g2, tm=256):
    n, d = x2.shape
    has_vres = vres is not None
    row_spec = pl.BlockSpec((tm, d), lambda i: (i, 0))
    prev_spec = pl.BlockSpec((8, d), lambda i: (jnp.maximum(i * (tm // 8) - 1, 0), 0))
    ops = [x2, x2]
    specs = [row_spec, prev_spec]
    if has_vres:
        ops.append(v_first)
        specs.append(row_spec)
    consts = [mu, wrkv, w0, w1, w2, a0, a1, a2]
    if has_vres:
        consts += list(vres)
    consts += [g1, g2]
    ops += consts
    specs += [_const_spec(c.shape) for c in consts]
    out = jax.ShapeDtypeStruct((n, d), F32)
    return pl.pallas_call(
        functools.partial(_rwkv_pre_kernel, has_vres, seq_len // tm),
        grid=(n // tm,),
        in_specs=specs,
        out_specs=[row_spec] * 6,
        out_shape=[out] * 6,
        compiler_params=_params(1),
        name="rwkv_pre",
    )(*ops)


def _wkv_kernel(n_sub, r_ref, lw_ref, k_ref, v_ref, al_ref, kk_ref, ka_ref, rk_ref, lg_ref,
                lb_ref, o_ref, st_ref):
    c = WKV_CHUNK
    gl = GROUP_LANES
    hd = RW_HEAD_DIM
    n_groups = r_ref.shape[1] // gl
    units = [(s, g) for s in range(n_sub) for g in range(n_groups)]

    @pl.when(pl.program_id(1) == 0)
    def _():
        st_ref[...] = jnp.zeros_like(st_ref)

    row = lax.broadcasted_iota(jnp.int32, (gl, gl), 0)
    col = lax.broadcasted_iota(jnp.int32, (gl, gl), 1)
    same_head = (row // hd) == (col // hd)
    head_sum = jnp.where(same_head, 1.0, 0.0).astype(BF16)
    t_row = lax.broadcasted_iota(jnp.int32, (c, gl), 0)
    t_col = lax.broadcasted_iota(jnp.int32, (c, gl), 1) & (hd - 1)
    tri_strict = t_row > t_col
    tri_incl = t_row >= t_col
    eye = jnp.where(t_row == t_col, 1.0, 0.0).astype(F32)
    crow = lax.broadcasted_iota(jnp.int32, (c, c), 0)
    ccol = lax.broadcasted_iota(jnp.int32, (c, c), 1)
    cum_mat = jnp.where(crow >= ccol, 1.0, 0.0).astype(BF16)
    lane = lax.broadcasted_iota(jnp.int32, (c, LANES), 1)
    half = [jnp.where(lane < hd, 1.0, 0.0).astype(BF16),
            jnp.where(lane >= hd, 1.0, 0.0).astype(BF16)]
    zero_slab = jnp.zeros((c, LANES), BF16)

    def block_diag(z):
        zb = z.astype(BF16)
        blocks = []
        for h in range(gl // hd):
            j = (h * hd) // LANES
            slab = zb[:, j * LANES:(j + 1) * LANES] * half[h % (LANES // hd)]
            pieces = [zero_slab] * (gl // LANES)
            pieces[j] = slab
            blocks.append(jnp.concatenate(pieces, axis=1))
        return jnp.concatenate(blocks, axis=0)

    def block_diag_t(z):
        zt = jnp.transpose(jnp.concatenate([z] * (LANES // c), axis=0)).astype(BF16)
        blocks = []
        for h in range(gl // hd):
            slab = zt[h * hd:(h + 1) * hd, :] * half[h % (LANES // hd)]
            pieces = [zero_slab] * (gl // LANES)
            pieces[(h * c) // LANES] = slab
            blocks.append(jnp.concatenate(pieces, axis=1))
        return jnp.concatenate(blocks, axis=0)

    def seg_sums(zs):
        out = _dot(jnp.concatenate([z.astype(BF16) for z in zs], axis=0), head_sum)
        return [out[i * c:(i + 1) * c] for i in range(len(zs))]

    groups = range(n_groups)
    state = [st_ref[g] for g in groups]

    def chunk_steps(s):
        rows = slice(s * c, (s + 1) * c)
        ds = []
        for g in groups:
            sl = slice(g * gl, (g + 1) * gl)
            r = r_ref[rows, sl]
            lw = lw_ref[rows, sl]
            k = k_ref[rows, sl]
            v = v_ref[rows, sl]
            al = al_ref[rows, sl]
            kk = k * kk_ref[:, sl]
            k2 = k * (1.0 + (al - 1.0) * ka_ref[:, sl])
            hi, mid, lo = _split3(lw)
            cum = _dot(cum_mat, hi) + _dot(cum_mat, mid) + _dot(cum_mat, lo)
            ds.append(dict(r=r, lw=lw, v=v, al=al, kk=kk, k2=k2, cum=cum,
                           bonus=r * k2 * rk_ref[:, sl]))
            yield
        sums = seg_sums([d["kk"] * d["kk"] for d in ds] + [d["bonus"] for d in ds])
        for g, d in enumerate(ds):
            d["ss"] = sums[g]
            d["bonus"] = sums[n_groups + g] * d["v"]
        yield
        for d in ds:
            kk = d["kk"] / jnp.maximum(jnp.sqrt(d["ss"]), 1e-12)
            b = kk * d["al"]
            cum, lw = d["cum"], d["lw"]
            w_incl = jnp.exp(cum)
            w_excl = jnp.exp(cum - lw)
            w_inv = jnp.exp(-cum)
            w_end = w_incl[c - 1:c, :]
            w_tail = w_inv * w_end
            d["w_end"] = w_end
            d["ar"] = jnp.concatenate(
                [(-kk * w_excl).astype(BF16), (d["r"] * w_incl).astype(BF16)], axis=0)
            d["bk_tail"] = jnp.concatenate(
                [(b * w_tail).astype(BF16), (d["k2"] * w_tail).astype(BF16)], axis=0)
            d["bk_bd"] = jnp.concatenate(
                [block_diag_t(b * w_inv), block_diag_t(d["k2"] * w_inv)], axis=1)
            d["v_bd"] = block_diag(d["v"])
            yield
        for d in ds:
            sc = _dot(d["ar"], d["bk_bd"])
            d["pw"] = jnp.where(tri_strict, sc[:c, :gl], 0.0)
            d["inv"] = eye + d["pw"]
            n_ak = jnp.where(tri_strict, sc[:c, gl:], 0.0).astype(BF16)
            d["n_rb"] = jnp.where(tri_incl, sc[c:, :gl], 0.0).astype(BF16)
            n_rk = jnp.where(tri_incl, sc[c:, gl:], 0.0).astype(BF16)
            kv = _dot(jnp.concatenate([n_ak, n_rk], axis=0), d["v_bd"])
            d["akv"] = kv[:c]
            d["rkv"] = kv[c:]
            yield
        levels = c.bit_length() - 1
        for l in range(levels):
            first, last = l == 0, l == levels - 1
            for d in ds:
                pwb = d["pw"].astype(BF16)
                lhs = ([] if first else [d["inv"].astype(BF16)]) + ([] if last else [pwb])
                out = _dot(jnp.concatenate(lhs, axis=0), block_diag(pwb))
                if not first:
                    d["inv"] = d["inv"] + out[:c]
                if not last:
                    d["pw"] = out[out.shape[0] - c:]
                yield
        assert chunks_done[0] == s
        for g, d in enumerate(ds):
            d["fs"] = _dot_nt(d["ar"], state[g].astype(BF16))
            yield
        for d in ds:
            d["u"] = _dot(d["inv"].astype(BF16), block_diag(d["fs"][:c] + d["akv"]))
            yield
        for g, d in enumerate(ds):
            u = d["u"]
            d["y"] = d["fs"][c:] + _dot(d["n_rb"], block_diag(u)) + d["rkv"]
            uv = jnp.concatenate([u.astype(BF16), d["v"].astype(BF16)], axis=0)
            upd = state[g] * d["w_end"] + _dot_tn(uv, d["bk_tail"])
            state[g] = jnp.where(same_head, upd, 0.0)
            if g == n_groups - 1:
                chunks_done[0] = s + 1
            yield
        inv_n = 1.0 / hd
        means = seg_sums([d["y"] for d in ds])
        dlts = [d["y"] - m * inv_n for d, m in zip(ds, means)]
        yield
        variances = seg_sums([dl * dl for dl in dlts])
        for g, (d, dl, var) in enumerate(zip(ds, dlts, variances)):
            sl = slice(g * gl, (g + 1) * gl)
            yn = dl * lax.rsqrt(var * inv_n + LNX_EPS) * lg_ref[:, sl] + lb_ref[:, sl]
            o_ref[rows, sl] = yn + d["bonus"]

    chunks_done = [0]
    _run_pipelined([chunk_steps(s) for s in range(n_sub)], skew=3 * n_groups)
    for g in groups:
        st_ref[g] = state[g]


def _wkv(r, lw, k, v, al, k_k, k_a, r_k, lnx_g, lnx_b, batch, seq_len, n_sub=8):
    n, d = r.shape
    rows = WKV_CHUNK * n_sub
    nc = seq_len // rows
    row_spec = pl.BlockSpec((rows, d), lambda bi, ci: (bi * nc + ci, 0))
    vec_spec = pl.BlockSpec((1, d), lambda bi, ci: (0, 0))
    return pl.pallas_call(
        functools.partial(_wkv_kernel, n_sub),
        grid=(batch, nc),
        in_specs=[row_spec] * 5 + [vec_spec] * 5,
        out_specs=row_spec,
        out_shape=jax.ShapeDtypeStruct((n, d), F32),
        scratch_shapes=[pltpu.VMEM((d // GROUP_LANES, GROUP_LANES, GROUP_LANES), F32)],
        compiler_params=_params(2),
        name="wkv",
    )(r, lw, k, v, al, k_k, k_a, r_k, lnx_g, lnx_b)


def _sg_kernel(alpha, x_ref, win_ref, bin_ref, lng_ref, lnb_ref, ws_ref, bs_ref, wout_ref,
               lg_ref, lb_ref, o_ref, gated_ref):
    tm = x_ref.shape[0]
    e = wout_ref.shape[0]
    n_groups = ws_ref.shape[0]
    gd = e // n_groups
    chunks = SG_SUBTILE // SG_CHUNK
    row = lax.broadcasted_iota(jnp.int32, (SG_CHUNK, SG_CHUNK), 0)
    col = lax.broadcasted_iota(jnp.int32, (SG_CHUNK, SG_CHUNK), 1)
    causal = row >= col
    groups_per_pass = max(1, MXU_COLS // gd)

    def gelu(z):
        half_z = 0.5 * z
        inner = z * (GELU_C0 + GELU_C1 * (z * z))
        return half_z + half_z * jnp.tanh(inner)

    def subtile_steps(t):
        rows = slice(t * SG_SUBTILE, (t + 1) * SG_SUBTILE)
        x = x_ref[rows, :]
        xb = x.astype(BF16)
        v_parts, row_sum = [], 0.0
        for lo in range(e, 2 * e, SG_V_COLS):
            cols = slice(lo, lo + SG_V_COLS)
            part = gelu(_dot(xb, win_ref[:, cols]) + bin_ref[:, cols])
            row_sum = row_sum + jnp.sum(part, axis=-1, keepdims=True)
            v_parts.append(part)
            yield
        mean = row_sum * (1.0 / e)
        sq_sum = 0.0
        for j in range(len(v_parts)):
            v_parts[j] = v_parts[j] - mean
            sq_sum = sq_sum + jnp.sum(v_parts[j] * v_parts[j], axis=-1, keepdims=True)
            yield
        rstd = lax.rsqrt(sq_sum * (1.0 / e) + LN_EPS)
        for g0 in range(0, n_groups, groups_per_pass):
            wide = slice(g0 * gd, (g0 + groups_per_pass) * gd)
            u = gelu(_dot(xb, win_ref[:, wide]) + bin_ref[:, wide])
            part = v_parts[(g0 * gd) // SG_V_COLS]
            off = (g0 * gd) % SG_V_COLS
            v = (part[:, off:off + groups_per_pass * gd] * rstd * lng_ref[:, wide]
                 + lnb_ref[:, wide]).astype(BF16)
            for gi in range(groups_per_pass):
                g = g0 + gi
                ws = jnp.where(causal, ws_ref[g], jnp.zeros((), BF16))
                cols = slice(g * gd, (g + 1) * gd)
                v_cat = jnp.concatenate(
                    [v[n * SG_CHUNK:(n + 1) * SG_CHUNK, gi * gd:(gi + 1) * gd]
                     for n in range(chunks)], axis=1)
                mixed = _dot(ws, v_cat)
                for n in range(chunks):
                    sub = slice(n * SG_CHUNK, (n + 1) * SG_CHUNK)
                    gate = mixed[:, n * gd:(n + 1) * gd] + bs_ref[g]
                    lo = t * SG_SUBTILE + n * SG_CHUNK
                    gated_ref[lo:lo + SG_CHUNK, cols] = (
                        u[sub, gi * gd:(gi + 1) * gd] * gate).astype(BF16)
            yield
        h = _dot(gated_ref[rows, :], wout_ref[...])
        yield
        o_ref[rows, :] = _layer_norm(alpha * x + h, lg_ref[...], lb_ref[...])

    _run_pipelined([subtile_steps(t) for t in range(tm // SG_SUBTILE)], SG_SKEW)


def _spatial_gating(x2, w_in, b_in, ln_g, ln_b, w_s, b_s, w_out, lg, lb, alpha, tm=1024):
    n, d = x2.shape
    e = w_out.shape[0]
    row_spec = pl.BlockSpec((tm, d), lambda i: (i, 0))
    consts = [w_in, b_in, ln_g, ln_b, w_s, b_s, w_out, lg, lb]
    return pl.pallas_call(
        functools.partial(_sg_kernel, alpha),
        grid=(n // tm,),
        in_specs=[row_spec] + [_const_spec(c.shape) for c in consts],
        out_specs=row_spec,
        out_shape=jax.ShapeDtypeStruct((n, d), F32),
        scratch_shapes=[pltpu.VMEM((tm, e), BF16)],
        compiler_params=_params(1),
        name="spatial_gating",
    )(x2, *consts)


def _ffn_kernel(alpha, f_chunk, has_mixer_out, *refs):
    if has_mixer_out:
        (x_ref, y_ref, g_ref, wo_ref, lg1_ref, lb1_ref,
         wgu_ref, wd_ref, lg_ref, lb_ref, o_ref, acc_ref) = refs
        h = _dot((y_ref[...] * g_ref[...]).astype(BF16), wo_ref[...])
        x = _layer_norm(alpha * x_ref[...] + h, lg1_ref[...], lb1_ref[...])
    else:
        x_ref, wgu_ref, wd_ref, lg_ref, lb_ref, o_ref, acc_ref = refs
        x = x_ref[...]
    xb = x.astype(BF16)
    f = wd_ref.shape[0]
    for ci in range(f // f_chunk):
        lo = ci * f_chunk
        gate = _dot(xb, wgu_ref[:, lo:lo + f_chunk])
        up = _dot(xb, wgu_ref[:, f + lo:f + lo + f_chunk])
        act = (gate * _sigmoid(gate) * up).astype(BF16)
        part = _dot(act, wd_ref[lo:lo + f_chunk, :])
        if ci == 0:
            acc_ref[...] = part
        else:
            acc_ref[...] += part
    o_ref[...] = _layer_norm(alpha * x + acc_ref[...], lg_ref[...], lb_ref[...])


def _ffn(x2, mixer_out, w_gate_up, w_down, lg, lb, alpha, tm=512, f_chunk=256):
    n, d = x2.shape
    row_spec = pl.BlockSpec((tm, d), lambda i: (i, 0))
    rows = [x2]
    consts = [w_gate_up, w_down, lg, lb]
    if mixer_out is not None:
        rows += list(mixer_out[:2])
        consts = list(mixer_out[2:]) + consts
    return pl.pallas_call(
        functools.partial(_ffn_kernel, alpha, f_chunk, mixer_out is not None),
        grid=(n // tm,),
        in_specs=[row_spec] * len(rows) + [_const_spec(c.shape) for c in consts],
        out_specs=row_spec,
        out_shape=jax.ShapeDtypeStruct((n, d), F32),
        scratch_shapes=[pltpu.VMEM((tm, d), F32)],
        compiler_params=_params(1),
        name="swiglu",
    )(*rows, *consts)


def kernel(x, rw_mu, rw_w_rkv, rw_w0, rw_w1, rw_w2, rw_a0, rw_a1, rw_a2, rw_v0, rw_v1, rw_v2, rw_g1, rw_g2, rw_k_k, rw_k_a, rw_r_k, rw_lnx_g, rw_lnx_b, rw_w_out, sg_w_in, sg_b_in, sg_ln_g, sg_ln_b, sg_w_s, sg_b_s, sg_w_out, ff_w_gate_up, ff_w_down, ln_g, ln_b):
    batch, seq_len, d = x.shape
    depth = ff_w_gate_up.shape[0]
    alpha = (2.0 * depth) ** 0.25
    n = batch * seq_len
    x2 = x.reshape(n, d)
    bf = lambda w: w.astype(BF16)
    rowv = lambda p: p.reshape(1, -1)

    v_first = None
    for i in range(depth):
        j = i // 2
        lg, lb = rowv(ln_g[i, 0]), rowv(ln_b[i, 0])
        if i % 2 == 0:
            vres = None if j == 0 else (rowv(rw_v0[j - 1]), bf(rw_v1[j - 1]), bf(rw_v2[j - 1]))
            r, lw, k, v, al, g = _rwkv_pre(
                x2, v_first, seq_len, rw_mu[j], bf(rw_w_rkv[j]), rowv(rw_w0[j]), bf(rw_w1[j]),
                bf(rw_w2[j]), rowv(rw_a0[j]), bf(rw_a1[j]), bf(rw_a2[j]), vres,
                bf(rw_g1[j]), bf(rw_g2[j]))
            if j == 0:
                v_first = v
            y = _wkv(r, lw, k, v, al, rowv(rw_k_k[j]), rowv(rw_k_a[j]), rowv(rw_r_k[j]),
                     rowv(rw_lnx_g[j]), rowv(rw_lnx_b[j]), batch, seq_len)
            mixer_out = (y, g, bf(rw_w_out[j]), lg, lb)
        else:
            mixer_out = None
            n_groups, chunk = sg_b_s.shape[1], sg_b_s.shape[2]
            bias = jnp.broadcast_to(sg_b_s[j][:, :, None], (n_groups, chunk, LANES))
            x2 = _spatial_gating(
                x2, bf(sg_w_in[j]), rowv(sg_b_in[j]), rowv(sg_ln_g[j]), rowv(sg_ln_b[j]),
                bf(sg_w_s[j]), bias, bf(sg_w_out[j]), lg, lb, alpha)
        x2 = _ffn(x2, mixer_out, bf(ff_w_gate_up[i]), bf(ff_w_down[i]),
                  rowv(ln_g[i, 1]), rowv(ln_b[i, 1]), alpha)
    return x2.reshape(batch, seq_len, d)
```

```python
import functools

import jax
import jax.numpy as jnp
from jax import lax
from jax.experimental import pallas as pl
from jax.experimental.pallas import tpu as pltpu

F32 = jnp.float32
BF16 = jnp.bfloat16

RW_HEAD_DIM = 64
LNX_EPS = 64e-5
LN_EPS = 1e-5
SG_CHUNK = 128
GELU_C0 = 0.7978845608028654
GELU_C1 = GELU_C0 * 0.044715

LANES = 128
MXU_COLS = 256
VMEM_LIMIT_BYTES = 56 * 1024 * 1024

WKV_CHUNK = 64
GROUP_LANES = MXU_COLS
SG_SUBTILE = 256
SG_V_COLS = 512
SG_SKEW = 8
assert WKV_CHUNK == RW_HEAD_DIM


def _dot(a, b):
    return jnp.dot(a, b, preferred_element_type=F32)


def _dot_nt(a, b):
    return lax.dot_general(a, b, (((1,), (1,)), ((), ())), preferred_element_type=F32)


def _dot_tn(a, b):
    return lax.dot_general(a, b, (((0,), (0,)), ((), ())), preferred_element_type=F32)


def _sigmoid(x):
    return 1.0 / (1.0 + jnp.exp(-x))


def _layer_norm(y, g, b):
    mean = jnp.mean(y, axis=-1, keepdims=True)
    d = y - mean
    var = jnp.mean(d * d, axis=-1, keepdims=True)
    return d * lax.rsqrt(var + LN_EPS) * g + b


def _split3(x):
    hi = x.astype(BF16)
    r1 = x - hi.astype(F32)
    mid = r1.astype(BF16)
    lo = (r1 - mid.astype(F32)).astype(BF16)
    return hi, mid, lo


def _run_pipelined(generators, skew):
    pending = [(i * skew, gen) for i, gen in enumerate(generators)]
    tick = 0
    while pending:
        pending = [(t0, gen) for t0, gen in pending
                   if tick < t0 or next(gen, "done") != "done"]
        tick += 1


def _const_spec(shape):
    nd = len(shape)
    return pl.BlockSpec(shape, lambda *_: (0,) * nd, pipeline_mode=pl.Buffered(1))


def _params(n_parallel):
    return pltpu.CompilerParams(
        dimension_semantics=("arbitrary",) * n_parallel,
        vmem_limit_bytes=VMEM_LIMIT_BYTES,
    )


def _rwkv_pre_kernel(has_vres, tiles_per_seq, *refs):
    if has_vres:
        (x_ref, xp_ref, vf_ref, mu_ref, wrkv_ref, w0_ref, w1_ref, w2_ref, a0_ref, a1_ref,
         a2_ref, v0_ref, v1_ref, v2_ref, g1_ref, g2_ref,
         r_out, lw_out, k_out, v_out, al_out, g_out) = refs
    else:
        (x_ref, xp_ref, mu_ref, wrkv_ref, w0_ref, w1_ref, w2_ref, a0_ref, a1_ref,
         a2_ref, g1_ref, g2_ref,
         r_out, lw_out, k_out, v_out, al_out, g_out) = refs
    i = pl.program_id(0)
    x = x_ref[...]
    prev_row = xp_ref[7:8, :] * jnp.where(i % tiles_per_seq == 0, 0.0, 1.0)
    row = lax.broadcasted_iota(jnp.int32, x.shape, 0)
    shifted = jnp.where(row == 0, prev_row, pltpu.roll(x, 1, 0))
    xx = shifted - x

    def mix(j):
        return (x + xx * mu_ref[j:j + 1, :]).astype(BF16)

    d = x.shape[1]
    col_blocks = [slice(lo, lo + MXU_COLS) for lo in range(0, d, MXU_COLS)]

    xr, xw = mix(0), mix(1)
    r_out[...] = _dot(xr, wrkv_ref[0])
    xk = mix(2)
    w_mid = jnp.tanh(_dot(xw, w1_ref[...])).astype(BF16)
    k_out[...] = _dot(xk, wrkv_ref[1])
    xv = mix(3)
    for cb in col_blocks:
        nz = -(w0_ref[:, cb] + _dot(w_mid, w2_ref[:, cb]))
        softplus = jnp.maximum(nz, 0.0) + jnp.log(1.0 + jnp.exp(-jnp.abs(nz)))
        lw_out[:, cb] = -jnp.exp(-softplus - 0.5)
    xa = mix(4)
    if has_vres:
        v_mid = _dot(xv, v1_ref[...]).astype(BF16)
    for cb in col_blocks:
        v = _dot(xv, wrkv_ref[2, :, cb])
        if has_vres:
            gate = _sigmoid(v0_ref[:, cb] + _dot(v_mid, v2_ref[:, cb]))
            v = v + (vf_ref[:, cb] - v) * gate
        v_out[:, cb] = v
    xg = mix(5)
    a_mid = _dot(xa, a1_ref[...]).astype(BF16)
    g_mid = _sigmoid(_dot(xg, g1_ref[...])).astype(BF16)
    for cb in col_blocks:
        al_out[:, cb] = _sigmoid(a0_ref[:, cb] + _dot(a_mid, a2_ref[:, cb]))
    g_out[...] = _dot(g_mid, g2_ref[...])


def _rwkv_pre(x2, v_first, seq_len, mu, wrkv, w0, w1, w2, a0, a1, a2, vres, g1, g2, tm=512):
    n, d = x2.shape
    has_vres = vres is not None
    row_spec = pl.BlockSpec((tm, d), lambda i: (i, 0))
    prev_spec = pl.BlockSpec((8, d), lambda i: (jnp.maximum(i * (tm // 8) - 1, 0), 0))
    ops = [x2, x2]
    specs = [row_spec, prev_spec]
    if has_vres:
        ops.append(v_first)
        specs.append(row_spec)
    consts = [mu, wrkv, w0, w1, w2, a0, a1, a2]
    if has_vres:
        consts += list(vres)
    consts += [g1, g2]
    ops += consts
    specs += [_const_spec(c.shape) for c in consts]
    out = jax.ShapeDtypeStruct((n, d), F32)
    return pl.pallas_call(
        functools.partial(_rwkv_pre_kernel, has_vres, seq_len // tm),
        grid=(n // tm,),
        in_specs=specs,
        out_specs=[row_spec] * 6,
        out_shape=[out] * 6,
        compiler_params=_params(1),
        name="rwkv_pre",
    )(*ops)


def _wkv_kernel(n_sub, r_ref, lw_ref, k_ref, v_ref, al_ref, kk_ref, ka_ref, rk_ref, lg_ref,
                lb_ref, o_ref, st_ref):
    c = WKV_CHUNK
    gl = GROUP_LANES
    hd = RW_HEAD_DIM
    n_groups = r_ref.shape[1] // gl
    units = [(s, g) for s in range(n_sub) for g in range(n_groups)]

    @pl.when(pl.program_id(1) == 0)
    def _():
        st_ref[...] = jnp.zeros_like(st_ref)

    row = lax.broadcasted_iota(jnp.int32, (gl, gl), 0)
    col = lax.broadcasted_iota(jnp.int32, (gl, gl), 1)
    same_head = (row // hd) == (col // hd)
    head_sum = jnp.where(same_head, 1.0, 0.0).astype(BF16)
    t_row = lax.broadcasted_iota(jnp.int32, (c, gl), 0)
    t_col = lax.broadcasted_iota(jnp.int32, (c, gl), 1) & (hd - 1)
    tri_strict = t_row > t_col
    tri_incl = t_row >= t_col
    eye = jnp.where(t_row == t_col, 1.0, 0.0).astype(F32)
    crow = lax.broadcasted_iota(jnp.int32, (c, c), 0)
    ccol = lax.broadcasted_iota(jnp.int32, (c, c), 1)
    cum_mat = jnp.where(crow >= ccol, 1.0, 0.0).astype(BF16)
    lane = lax.broadcasted_iota(jnp.int32, (c, LANES), 1)
    half = [jnp.where(lane < hd, 1.0, 0.0).astype(BF16),
            jnp.where(lane >= hd, 1.0, 0.0).astype(BF16)]
    zero_slab = jnp.zeros((c, LANES), BF16)

    def block_diag(z):
        zb = z.astype(BF16)
        blocks = []
        for h in range(gl // hd):
            j = (h * hd) // LANES
            slab = zb[:, j * LANES:(j + 1) * LANES] * half[h % (LANES // hd)]
            pieces = [zero_slab] * (gl // LANES)
            pieces[j] = slab
            blocks.append(jnp.concatenate(pieces, axis=1))
        return jnp.concatenate(blocks, axis=0)

    def block_diag_t(z):
        zt = jnp.transpose(jnp.concatenate([z] * (LANES // c), axis=0)).astype(BF16)
        blocks = []
        for h in range(gl // hd):
            slab = zt[h * hd:(h + 1) * hd, :] * half[h % (LANES // hd)]
            pieces = [zero_slab] * (gl // LANES)
            pieces[(h * c) // LANES] = slab
            blocks.append(jnp.concatenate(pieces, axis=1))
        return jnp.concatenate(blocks, axis=0)

    def seg_sums(zs):
        out = _dot(jnp.concatenate([z.astype(BF16) for z in zs], axis=0), head_sum)
        return [out[i * c:(i + 1) * c] for i in range(len(zs))]

    groups = range(n_groups)
    state = [st_ref[g] for g in groups]

    def chunk_steps(s):
        rows = slice(s * c, (s + 1) * c)
        ds = []
        for g in groups:
            sl = slice(g * gl, (g + 1) * gl)
            r = r_ref[rows, sl]
            lw = lw_ref[rows, sl]
            k = k_ref[rows, sl]
            v = v_ref[rows, sl]
            al = al_ref[rows, sl]
            kk = k * kk_ref[:, sl]
            k2 = k * (1.0 + (al - 1.0) * ka_ref[:, sl])
            hi, mid, lo = _split3(lw)
            cum = _dot(cum_mat, hi) + _dot(cum_mat, mid) + _dot(cum_mat, lo)
            ds.append(dict(r=r, lw=lw, v=v, al=al, kk=kk, k2=k2, cum=cum,
                           bonus=r * k2 * rk_ref[:, sl]))
            yield
        sums = seg_sums([d["kk"] * d["kk"] for d in ds] + [d["bonus"] for d in ds])
        for g, d in enumerate(ds):
            d["ss"] = sums[g]
            d["bonus"] = sums[n_groups + g] * d["v"]
        yield
        for d in ds:
            kk = d["kk"] / jnp.maximum(jnp.sqrt(d["ss"]), 1e-12)
            b = kk * d["al"]
            cum, lw = d["cum"], d["lw"]
            w_incl = jnp.exp(cum)
            w_excl = jnp.exp(cum - lw)
            w_inv = jnp.exp(-cum)
            w_end = w_incl[c - 1:c, :]
            w_tail = w_inv * w_end
            d["w_end"] = w_end
            d["ar"] = jnp.concatenate(
                [(-kk * w_excl).astype(BF16), (d["r"] * w_incl).astype(BF16)], axis=0)
            d["bk_tail"] = jnp.concatenate(
                [(b * w_tail).astype(BF16), (d["k2"] * w_tail).astype(BF16)], axis=0)
            d["bk_bd"] = jnp.concatenate(
                [block_diag_t(b * w_inv), block_diag_t(d["k2"] * w_inv)], axis=1)
            d["v_bd"] = block_diag(d["v"])
            yield
        for d in ds:
            sc = _dot(d["ar"], d["bk_bd"])
            d["pw"] = jnp.where(tri_strict, sc[:c, :gl], 0.0)
            d["inv"] = eye + d["pw"]
            n_ak = jnp.where(tri_strict, sc[:c, gl:], 0.0).astype(BF16)
            d["n_rb"] = jnp.where(tri_incl, sc[c:, :gl], 0.0).astype(BF16)
            n_rk = jnp.where(tri_incl, sc[c:, gl:], 0.0).astype(BF16)
            kv = _dot(jnp.concatenate([n_ak, n_rk], axis=0), d["v_bd"])
            d["akv"] = kv[:c]
            d["rkv"] = kv[c:]
            yield
        levels = c.bit_length() - 1
        for l in range(levels):
            first, last = l == 0, l == levels - 1
            for d in ds:
                pwb = d["pw"].astype(BF16)
                lhs = ([] if first else [d["inv"].astype(BF16)]) + ([] if last else [pwb])
                out = _dot(jnp.concatenate(lhs, axis=0), block_diag(pwb))
                if not first:
                    d["inv"] = d["inv"] + out[:c]
                if not last:
                    d["pw"] = out[out.shape[0] - c:]
                yield
        assert chunks_done[0] == s
        for g, d in enumerate(ds):
            d["fs"] = _dot_nt(d["ar"], state[g].astype(BF16))
            yield
        for d in ds:
            d["u"] = _dot(d["inv"].astype(BF16), block_diag(d["fs"][:c] + d["akv"]))
            yield
        for g, d in enumerate(ds):
            u = d["u"]
            d["y"] = d["fs"][c:] + _dot(d["n_rb"], block_diag(u)) + d["rkv"]
            uv = jnp.concatenate([u.astype(BF16), d["v"].astype(BF16)], axis=0)
            upd = state[g] * d["w_end"] + _dot_tn(uv, d["bk_tail"])
            state[g] = jnp.where(same_head, upd, 0.0)
            if g == n_groups - 1:
                chunks_done[0] = s + 1
            yield
        inv_n = 1.0 / hd
        means = seg_sums([d["y"] for d in ds])
        dlts = [d["y"] - m * inv_n for d, m in zip(ds, means)]
        yield
        variances = seg_sums([dl * dl for dl in dlts])
        for g, (d, dl, var) in enumerate(zip(ds, dlts, variances)):
            sl = slice(g * gl, (g + 1) * gl)
            yn = dl * lax.rsqrt(var * inv_n + LNX_EPS) * lg_ref[:, sl] + lb_ref[:, sl]
            o_ref[rows, sl] = yn + d["bonus"]

    chunks_done = [0]
    _run_pipelined([chunk_steps(s) for s in range(n_sub)], skew=3 * n_groups)
    for g in groups:
        st_ref[g] = state[g]


def _wkv(r, lw, k, v, al, k_k, k_a, r_k, lnx_g, lnx_b, batch, seq_len, n_sub=8):
    n, d = r.shape
    rows = WKV_CHUNK * n_sub
    nc = seq_len // rows
    row_spec = pl.BlockSpec((rows, d), lambda bi, ci: (bi * nc + ci, 0))
    vec_spec = pl.BlockSpec((1, d), lambda bi, ci: (0, 0))
    return pl.pallas_call(
        functools.partial(_wkv_kernel, n_sub),
        grid=(batch, nc),
        in_specs=[row_spec] * 5 + [vec_spec] * 5,
        out_specs=row_spec,
        out_shape=jax.ShapeDtypeStruct((n, d), F32),
        scratch_shapes=[pltpu.VMEM((d // GROUP_LANES, GROUP_LANES, GROUP_LANES), F32)],
        compiler_params=_params(2),
        name="wkv",
    )(r, lw, k, v, al, k_k, k_a, r_k, lnx_g, lnx_b)


def _sg_kernel(alpha, x_ref, win_ref, bin_ref, lng_ref, lnb_ref, ws_ref, bs_ref, wout_ref,
               lg_ref, lb_ref, o_ref, gated_ref):
    tm = x_ref.shape[0]
    e = wout_ref.shape[0]
    n_groups = ws_ref.shape[0]
    gd = e // n_groups
    chunks = SG_SUBTILE // SG_CHUNK
    row = lax.broadcasted_iota(jnp.int32, (SG_CHUNK, SG_CHUNK), 0)
    col = lax.broadcasted_iota(jnp.int32, (SG_CHUNK, SG_CHUNK), 1)
    causal = row >= col
    groups_per_pass = max(1, MXU_COLS // gd)

    def gelu(z):
        half_z = 0.5 * z
        inner = z * (GELU_C0 + GELU_C1 * (z * z))
        return half_z + half_z * jnp.tanh(inner)

    def subtile_steps(t):
        rows = slice(t * SG_SUBTILE, (t + 1) * SG_SUBTILE)
        x = x_ref[rows, :]
        xb = x.astype(BF16)
        v_parts, row_sum = [], 0.0
        for lo in range(e, 2 * e, SG_V_COLS):
            cols = slice(lo, lo + SG_V_COLS)
            part = gelu(_dot(xb, win_ref[:, cols]) + bin_ref[:, cols])
            row_sum = row_sum + jnp.sum(part, axis=-1, keepdims=True)
            v_parts.append(part)
            yield
        mean = row_sum * (1.0 / e)
        sq_sum = 0.0
        for j in range(len(v_parts)):
            v_parts[j] = v_parts[j] - mean
            sq_sum = sq_sum + jnp.sum(v_parts[j] * v_parts[j], axis=-1, keepdims=True)
            yield
        rstd = lax.rsqrt(sq_sum * (1.0 / e) + LN_EPS)
        for g0 in range(0, n_groups, groups_per_pass):
            wide = slice(g0 * gd, (g0 + groups_per_pass) * gd)
            u = gelu(_dot(xb, win_ref[:, wide]) + bin_ref[:, wide])
            part = v_parts[(g0 * gd) // SG_V_COLS]
            off = (g0 * gd) % SG_V_COLS
            v = (part[:, off:off + groups_per_pass * gd] * rstd * lng_ref[:, wide]
                 + lnb_ref[:, wide]).astype(BF16)
            for gi in range(groups_per_pass):
                g = g0 + gi
                ws = jnp.where(causal, ws_ref[g], jnp.zeros((), BF16))
                cols = slice(g * gd, (g + 1) * gd)
                v_cat = jnp.concatenate(
                    [v[n * SG_CHUNK:(n + 1) * SG_CHUNK, gi * gd:(gi + 1) * gd]
                     for n in range(chunks)], axis=1)
                mixed = _dot(ws, v_cat)
                for n in range(chunks):
                    sub = slice(n * SG_CHUNK, (n + 1) * SG_CHUNK)
                    gate = mixed[:, n * gd:(n + 1) * gd] + bs_ref[g]
                    lo = t * SG_SUBTILE + n * SG_CHUNK
                    gated_ref[lo:lo + SG_CHUNK, cols] = (
                        u[sub, gi * gd:(gi + 1) * gd] * gate).astype(BF16)
            yield
        h = _dot(gated_ref[rows, :], wout_ref[...])
        yield
        o_ref[rows, :] = _layer_norm(alpha * x + h, lg_ref[...], lb_ref[...])

    _run_pipelined([subtile_steps(t) for t in range(tm // SG_SUBTILE)], SG_SKEW)


def _spatial_gating(x2, w_in, b_in, ln_g, ln_b, w_s, b_s, w_out, lg, lb, alpha, tm=1024):
    n, d = x2.shape
    e = w_out.shape[0]
    row_spec = pl.BlockSpec((tm, d), lambda i: (i, 0))
    consts = [w_in, b_in, ln_g, ln_b, w_s, b_s, w_out, lg, lb]
    return pl.pallas_call(
        functools.partial(_sg_kernel, alpha),
        grid=(n // tm,),
        in_specs=[row_spec] + [_const_spec(c.shape) for c in consts],
        out_specs=row_spec,
        out_shape=jax.ShapeDtypeStruct((n, d), F32),
        scratch_shapes=[pltpu.VMEM((tm, e), BF16)],
        compiler_params=_params(1),
        name="spatial_gating",
    )(x2, *consts)


def _ffn_kernel(alpha, f_chunk, has_mixer_out, *refs):
    if has_mixer_out:
        (x_ref, y_ref, g_ref, wo_ref, lg1_ref, lb1_ref,
         wgu_ref, wd_ref, lg_ref, lb_ref, o_ref, acc_ref) = refs
        h = _dot((y_ref[...] * g_ref[...]).astype(BF16), wo_ref[...])
        x = _layer_norm(alpha * x_ref[...] + h, lg1_ref[...], lb1_ref[...])
    else:
        x_ref, wgu_ref, wd_ref, lg_ref, lb_ref, o_ref, acc_ref = refs
        x = x_ref[...]
    xb = x.astype(BF16)
    f = wd_ref.shape[0]
    for ci in range(f // f_chunk):
        lo = ci * f_chunk
        gate = _dot(xb, wgu_ref[:, lo:lo + f_chunk])
        up = _dot(xb, wgu_ref[:, f + lo:f + lo + f_chunk])
        act = (gate * _sigmoid(gate) * up).astype(BF16)
        part = _dot(act, wd_ref[lo:lo + f_chunk, :])
        if ci == 0:
            acc_ref[...] = part
        else:
            acc_ref[...] += part
    o_ref[...] = _layer_norm(alpha * x + acc_ref[...], lg_ref[...], lb_ref[...])


def _ffn(x2, mixer_out, w_gate_up, w_down, lg, lb, alpha, tm=512, f_chunk=256):
    n, d = x2.shape
    row_spec = pl.BlockSpec((tm, d), lambda i: (i, 0))
    rows = [x2]
    consts = [w_gate_up, w_down, lg, lb]
    if mixer_out is not None:
        rows += list(mixer_out[:2])
        consts = list(mixer_out[2:]) + consts
    return pl.pallas_call(
        functools.partial(_ffn_kernel, alpha, f_chunk, mixer_out is not None),
        grid=(n // tm,),
        in_specs=[row_spec] * len(rows) + [_const_spec(c.shape) for c in consts],
        out_specs=row_spec,
        out_shape=jax.ShapeDtypeStruct((n, d), F32),
        scratch_shapes=[pltpu.VMEM((tm, d), F32)],
        compiler_params=_params(1),
        name="swiglu",
    )(*rows, *consts)


def kernel(x, rw_mu, rw_w_rkv, rw_w0, rw_w1, rw_w2, rw_a0, rw_a1, rw_a2, rw_v0, rw_v1, rw_v2, rw_g1, rw_g2, rw_k_k, rw_k_a, rw_r_k, rw_lnx_g, rw_lnx_b, rw_w_out, sg_w_in, sg_b_in, sg_ln_g, sg_ln_b, sg_w_s, sg_b_s, sg_w_out, ff_w_gate_up, ff_w_down, ln_g, ln_b):
    batch, seq_len, d = x.shape
    depth = ff_w_gate_up.shape[0]
    alpha = (2.0 * depth) ** 0.25
    n = batch * seq_len
    x2 = x.reshape(n, d)
    bf = lambda w: w.astype(BF16)
    rowv = lambda p: p.reshape(1, -1)

    v_first = None
    for i in range(depth):
        j = i // 2
        lg, lb = rowv(ln_g[i, 0]), rowv(ln_b[i, 0])
        if i % 2 == 0:
            vres = None if j == 0 else (rowv(rw_v0[j - 1]), bf(rw_v1[j - 1]), bf(rw_v2[j - 1]))
            r, lw, k, v, al, g = _rwkv_pre(
                x2, v_first, seq_len, rw_mu[j], bf(rw_w_rkv[j]), rowv(rw_w0[j]), bf(rw_w1[j]),
                bf(rw_w2[j]), rowv(rw_a0[j]), bf(rw_a1[j]), bf(rw_a2[j]), vres,
                bf(rw_g1[j]), bf(rw_g2[j]))
            if j == 0:
                v_first = v
            y = _wkv(r, lw, k, v, al, rowv(rw_k_k[j]), rowv(rw_k_a[j]), rowv(rw_r_k[j]),
                     rowv(rw_lnx_g[j]), rowv(rw_lnx_b[j]), batch, seq_len)
            mixer_out = (y, g, bf(rw_w_out[j]), lg, lb)
        else:
            mixer_out = None
            n_groups, chunk = sg_b_s.shape[1], sg_b_s.shape[2]
            bias = jnp.broadcast_to(sg_b_s[j][:, :, None], (n_groups, chunk, LANES))
            x2 = _spatial_gating(
                x2, bf(sg_w_in[j]), rowv(sg_b_in[j]), rowv(sg_ln_g[j]), rowv(sg_ln_b[j]),
                bf(sg_w_s[j]), bias, bf(sg_w_out[j]), lg, lb, alpha)
        x2 = _ffn(x2, mixer_out, bf(ff_w_gate_up[i]), bf(ff_w_down[i]),
                  rowv(ln_g[i, 1]), rowv(ln_b[i, 1]), alpha)
    return x2.reshape(batch, seq_len, d)
```

```python
import functools

import jax
import jax.numpy as jnp
from jax import lax
from jax.experimental import pallas as pl
from jax.experimental.pallas import tpu as pltpu

F32 = jnp.float32
BF16 = jnp.bfloat16

RW_HEAD_DIM = 64
LNX_EPS = 64e-5
LN_EPS = 1e-5
SG_CHUNK = 128
GELU_C0 = 0.7978845608028654
GELU_C1 = GELU_C0 * 0.044715

LANES = 128
MXU_COLS = 256
VMEM_LIMIT_BYTES = 56 * 1024 * 1024

WKV_CHUNK = 64
GROUP_LANES = MXU_COLS
SG_SUBTILE = 256
SG_V_COLS = 512
SG_SKEW = 1
assert WKV_CHUNK == RW_HEAD_DIM


def _dot(a, b):
    return jnp.dot(a, b, preferred_element_type=F32)


def _dot_nt(a, b):
    return lax.dot_general(a, b, (((1,), (1,)), ((), ())), preferred_element_type=F32)


def _dot_tn(a, b):
    return lax.dot_general(a, b, (((0,), (0,)), ((), ())), preferred_element_type=F32)


def _sigmoid(x):
    return 1.0 / (1.0 + jnp.exp(-x))


def _layer_norm(y, g, b):
    mean = jnp.mean(y, axis=-1, keepdims=True)
    d = y - mean
    var = jnp.mean(d * d, axis=-1, keepdims=True)
    return d * lax.rsqrt(var + LN_EPS) * g + b


def _split3(x):
    hi = x.astype(BF16)
    r1 = x - hi.astype(F32)
    mid = r1.astype(BF16)
    lo = (r1 - mid.astype(F32)).astype(BF16)
    return hi, mid, lo


def _run_pipelined(generators, skew):
    pending = [(i * skew, gen) for i, gen in enumerate(generators)]
    tick = 0
    while pending:
        pending = [(t0, gen) for t0, gen in pending
                   if tick < t0 or next(gen, "done") != "done"]
        tick += 1


def _const_spec(shape):
    nd = len(shape)
    return pl.BlockSpec(shape, lambda *_: (0,) * nd, pipeline_mode=pl.Buffered(1))


def _params(n_parallel):
    return pltpu.CompilerParams(
        dimension_semantics=("arbitrary",) * n_parallel,
        vmem_limit_bytes=VMEM_LIMIT_BYTES,
    )


def _rwkv_pre_kernel(has_vres, tiles_per_seq, *refs):
    if has_vres:
        (x_ref, xp_ref, vf_ref, mu_ref, wrkv_ref, w0_ref, w1_ref, w2_ref, a0_ref, a1_ref,
         a2_ref, v0_ref, v1_ref, v2_ref, g1_ref, g2_ref,
         r_out, lw_out, k_out, v_out, al_out, g_out) = refs
    else:
        (x_ref, xp_ref, mu_ref, wrkv_ref, w0_ref, w1_ref, w2_ref, a0_ref, a1_ref,
         a2_ref, g1_ref, g2_ref,
         r_out, lw_out, k_out, v_out, al_out, g_out) = refs
    i = pl.program_id(0)
    x = x_ref[...]
    prev_row = xp_ref[7:8, :] * jnp.where(i % tiles_per_seq == 0, 0.0, 1.0)
    row = lax.broadcasted_iota(jnp.int32, x.shape, 0)
    shifted = jnp.where(row == 0, prev_row, pltpu.roll(x, 1, 0))
    xx = shifted - x

    def mix(j):
        return (x + xx * mu_ref[j:j + 1, :]).astype(BF16)

    d = x.shape[1]
    col_blocks = [slice(lo, lo + MXU_COLS) for lo in range(0, d, MXU_COLS)]

    xr, xw = mix(0), mix(1)
    r_out[...] = _dot(xr, wrkv_ref[0])
    xk = mix(2)
    w_mid = jnp.tanh(_dot(xw, w1_ref[...])).astype(BF16)
    k_out[...] = _dot(xk, wrkv_ref[1])
    xv = mix(3)
    for cb in col_blocks:
        nz = -(w0_ref[:, cb] + _dot(w_mid, w2_ref[:, cb]))
        softplus = jnp.maximum(nz, 0.0) + jnp.log(1.0 + jnp.exp(-jnp.abs(nz)))
        lw_out[:, cb] = -jnp.exp(-softplus - 0.5)
    xa = mix(4)
    if has_vres:
        v_mid = _dot(xv, v1_ref[...]).astype(BF16)
    for cb in col_blocks:
        v = _dot(xv, wrkv_ref[2, :, cb])
        if has_vres:
            gate = _sigmoid(v0_ref[:, cb] + _dot(v_mid, v2_ref[:, cb]))
            v = v + (vf_ref[:, cb] - v) * gate
        v_out[:, cb] = v
    xg = mix(5)
    a_mid = _dot(xa, a1_ref[...]).astype(BF16)
    g_mid = _sigmoid(_dot(xg, g1_ref[...])).astype(BF16)
    for cb in col_blocks:
        al_out[:, cb] = _sigmoid(a0_ref[:, cb] + _dot(a_mid, a2_ref[:, cb]))
    g_out[...] = _dot(g_mid, g2_ref[...])


def _rwkv_pre(x2, v_first, seq_len, mu, wrkv, w0, w1, w2, a0, a1, a2, vres, g1, g2, tm=512):
    n, d = x2.shape
    has_vres = vres is not None
    row_spec = pl.BlockSpec((tm, d), lambda i: (i, 0))
    prev_spec = pl.BlockSpec((8, d), lambda i: (jnp.maximum(i * (tm // 8) - 1, 0), 0))
    ops = [x2, x2]
    specs = [row_spec, prev_spec]
    if has_vres:
        ops.append(v_first)
        specs.append(row_spec)
    consts = [mu, wrkv, w0, w1, w2, a0, a1, a2]
    if has_vres:
        consts += list(vres)
    consts += [g1, g2]
    ops += consts
    specs += [_const_spec(c.shape) for c in consts]
    out = jax.ShapeDtypeStruct((n, d), F32)
    return pl.pallas_call(
        functools.partial(_rwkv_pre_kernel, has_vres, seq_len // tm),
        grid=(n // tm,),
        in_specs=specs,
        out_specs=[row_spec] * 6,
        out_shape=[out] * 6,
        compiler_params=_params(1),
        name="rwkv_pre",
    )(*ops)


def _wkv_kernel(n_sub, r_ref, lw_ref, k_ref, v_ref, al_ref, kk_ref, ka_ref, rk_ref, lg_ref,
                lb_ref, o_ref, st_ref):
    c = WKV_CHUNK
    gl = GROUP_LANES
    hd = RW_HEAD_DIM
    n_groups = r_ref.shape[1] // gl
    units = [(s, g) for s in range(n_sub) for g in range(n_groups)]

    @pl.when(pl.program_id(1) == 0)
    def _():
        st_ref[...] = jnp.zeros_like(st_ref)

    row = lax.broadcasted_iota(jnp.int32, (gl, gl), 0)
    col = lax.broadcasted_iota(jnp.int32, (gl, gl), 1)
    same_head = (row // hd) == (col // hd)
    head_sum = jnp.where(same_head, 1.0, 0.0).astype(BF16)
    t_row = lax.broadcasted_iota(jnp.int32, (c, gl), 0)
    t_col = lax.broadcasted_iota(jnp.int32, (c, gl), 1) & (hd - 1)
    tri_strict = t_row > t_col
    tri_incl = t_row >= t_col
    eye = jnp.where(t_row == t_col, 1.0, 0.0).astype(F32)
    crow = lax.broadcasted_iota(jnp.int32, (c, c), 0)
    ccol = lax.broadcasted_iota(jnp.int32, (c, c), 1)
    cum_mat = jnp.where(crow >= ccol, 1.0, 0.0).astype(BF16)
    lane = lax.broadcasted_iota(jnp.int32, (c, LANES), 1)
    half = [jnp.where(lane < hd, 1.0, 0.0).astype(BF16),
            jnp.where(lane >= hd, 1.0, 0.0).astype(BF16)]
    zero_slab = jnp.zeros((c, LANES), BF16)

    def block_diag(z):
        zb = z.astype(BF16)
        blocks = []
        for h in range(gl // hd):
            j = (h * hd) // LANES
            slab = zb[:, j * LANES:(j + 1) * LANES] * half[h % (LANES // hd)]
            pieces = [zero_slab] * (gl // LANES)
            pieces[j] = slab
            blocks.append(jnp.concatenate(pieces, axis=1))
        return jnp.concatenate(blocks, axis=0)

    def block_diag_t(z):
        zt = jnp.transpose(jnp.concatenate([z] * (LANES // c), axis=0)).astype(BF16)
        blocks = []
        for h in range(gl // hd):
            slab = zt[h * hd:(h + 1) * hd, :] * half[h % (LANES // hd)]
            pieces = [zero_slab] * (gl // LANES)
            pieces[(h * c) // LANES] = slab
            blocks.append(jnp.concatenate(pieces, axis=1))
        return jnp.concatenate(blocks, axis=0)

    def seg_sums(zs):
        out = _dot(jnp.concatenate([z.astype(BF16) for z in zs], axis=0), head_sum)
        return [out[i * c:(i + 1) * c] for i in range(len(zs))]

    groups = range(n_groups)
    state = [st_ref[g] for g in groups]

    def chunk_steps(s):
        rows = slice(s * c, (s + 1) * c)
        ds = []
        for g in groups:
            sl = slice(g * gl, (g + 1) * gl)
            r = r_ref[rows, sl]
            lw = lw_ref[rows, sl]
            k = k_ref[rows, sl]
            v = v_ref[rows, sl]
            al = al_ref[rows, sl]
            kk = k * kk_ref[:, sl]
            k2 = k * (1.0 + (al - 1.0) * ka_ref[:, sl])
            hi, mid, lo = _split3(lw)
            cum = _dot(cum_mat, hi) + _dot(cum_mat, mid) + _dot(cum_mat, lo)
            ds.append(dict(r=r, lw=lw, v=v, al=al, kk=kk, k2=k2, cum=cum,
                           bonus=r * k2 * rk_ref[:, sl]))
            yield
        sums = seg_sums([d["kk"] * d["kk"] for d in ds] + [d["bonus"] for d in ds])
        for g, d in enumerate(ds):
            d["ss"] = sums[g]
            d["bonus"] = sums[n_groups + g] * d["v"]
        yield
        for d in ds:
            kk = d["kk"] / jnp.maximum(jnp.sqrt(d["ss"]), 1e-12)
            b = kk * d["al"]
            cum, lw = d["cum"], d["lw"]
            w_incl = jnp.exp(cum)
            w_excl = jnp.exp(cum - lw)
            w_inv = jnp.exp(-cum)
            w_end = w_incl[c - 1:c, :]
            w_tail = w_inv * w_end
            d["w_end"] = w_end
            d["ar"] = jnp.concatenate(
                [(-kk * w_excl).astype(BF16), (d["r"] * w_incl).astype(BF16)], axis=0)
            d["bk_tail"] = jnp.concatenate(
                [(b * w_tail).astype(BF16), (d["k2"] * w_tail).astype(BF16)], axis=0)
            d["bk_bd"] = jnp.concatenate(
                [block_diag_t(b * w_inv), block_diag_t(d["k2"] * w_inv)], axis=1)
            d["v_bd"] = block_diag(d["v"])
            yield
        for d in ds:
            sc = _dot(d["ar"], d["bk_bd"])
            d["pw"] = jnp.where(tri_strict, sc[:c, :gl], 0.0)
            d["inv"] = eye + d["pw"]
            n_ak = jnp.where(tri_strict, sc[:c, gl:], 0.0).astype(BF16)
            d["n_rb"] = jnp.where(tri_incl, sc[c:, :gl], 0.0).astype(BF16)
            n_rk = jnp.where(tri_incl, sc[c:, gl:], 0.0).astype(BF16)
            kv = _dot(jnp.concatenate([n_ak, n_rk], axis=0), d["v_bd"])
            d["akv"] = kv[:c]
            d["rkv"] = kv[c:]
            yield
        levels = c.bit_length() - 1
        for l in range(levels):
            first, last = l == 0, l == levels - 1
            for d in ds:
                pwb = d["pw"].astype(BF16)
                lhs = ([] if first else [d["inv"].astype(BF16)]) + ([] if last else [pwb])
                out = _dot(jnp.concatenate(lhs, axis=0), block_diag(pwb))
                if not first:
                    d["inv"] = d["inv"] + out[:c]
                if not last:
                    d["pw"] = out[out.shape[0] - c:]
                yield
        assert chunks_done[0] == s
        for g, d in enumerate(ds):
            d["fs"] = _dot_nt(d["ar"], state[g].astype(BF16))
            yield
        for d in ds:
            d["u"] = _dot(d["inv"].astype(BF16), block_diag(d["fs"][:c] + d["akv"]))
            yield
        for g, d in enumerate(ds):
            u = d["u"]
            d["y"] = d["fs"][c:] + _dot(d["n_rb"], block_diag(u)) + d["rkv"]
            uv = jnp.concatenate([u.astype(BF16), d["v"].astype(BF16)], axis=0)
            upd = state[g] * d["w_end"] + _dot_tn(uv, d["bk_tail"])
            state[g] = jnp.where(same_head, upd, 0.0)
            if g == n_groups - 1:
                chunks_done[0] = s + 1
            yield
        inv_n = 1.0 / hd
        means = seg_sums([d["y"] for d in ds])
        dlts = [d["y"] - m * inv_n for d, m in zip(ds, means)]
        yield
        variances = seg_sums([dl * dl for dl in dlts])
        for g, (d, dl, var) in enumerate(zip(ds, dlts, variances)):
            sl = slice(g * gl, (g + 1) * gl)
            yn = dl * lax.rsqrt(var * inv_n + LNX_EPS) * lg_ref[:, sl] + lb_ref[:, sl]
            o_ref[rows, sl] = yn + d["bonus"]

    chunks_done = [0]
    _run_pipelined([chunk_steps(s) for s in range(n_sub)], skew=3 * n_groups)
    for g in groups:
        st_ref[g] = state[g]


def _wkv(r, lw, k, v, al, k_k, k_a, r_k, lnx_g, lnx_b, batch, seq_len, n_sub=8):
    n, d = r.shape
    rows = WKV_CHUNK * n_sub
    nc = seq_len // rows
    row_spec = pl.BlockSpec((rows, d), lambda bi, ci: (bi * nc + ci, 0))
    vec_spec = pl.BlockSpec((1, d), lambda bi, ci: (0, 0))
    return pl.pallas_call(
        functools.partial(_wkv_kernel, n_sub),
        grid=(batch, nc),
        in_specs=[row_spec] * 5 + [vec_spec] * 5,
        out_specs=row_spec,
        out_shape=jax.ShapeDtypeStruct((n, d), F32),
        scratch_shapes=[pltpu.VMEM((d // GROUP_LANES, GROUP_LANES, GROUP_LANES), F32)],
        compiler_params=_params(2),
        name="wkv",
    )(r, lw, k, v, al, k_k, k_a, r_k, lnx_g, lnx_b)


def _sg_kernel(alpha, x_ref, win_ref, bin_ref, lng_ref, lnb_ref, ws_ref, bs_ref, wout_ref,
               lg_ref, lb_ref, o_ref, gated_ref):
    tm = x_ref.shape[0]
    e = wout_ref.shape[0]
    n_groups = ws_ref.shape[0]
    gd = e // n_groups
    chunks = SG_SUBTILE // SG_CHUNK
    row = lax.broadcasted_iota(jnp.int32, (SG_CHUNK, SG_CHUNK), 0)
    col = lax.broadcasted_iota(jnp.int32, (SG_CHUNK, SG_CHUNK), 1)
    causal = row >= col
    groups_per_pass = max(1, MXU_COLS // gd)

    def gelu(z):
        half_z = 0.5 * z
        inner = z * (GELU_C0 + GELU_C1 * (z * z))
        return half_z + half_z * jnp.tanh(inner)

    def subtile_steps(t):
        rows = slice(t * SG_SUBTILE, (t + 1) * SG_SUBTILE)
        x = x_ref[rows, :]
        xb = x.astype(BF16)
        v_parts, row_sum = [], 0.0
        for lo in range(e, 2 * e, SG_V_COLS):
            cols = slice(lo, lo + SG_V_COLS)
            part = gelu(_dot(xb, win_ref[:, cols]) + bin_ref[:, cols])
            row_sum = row_sum + jnp.sum(part, axis=-1, keepdims=True)
            v_parts.append(part)
            yield
        mean = row_sum * (1.0 / e)
        sq_sum = 0.0
        for j in range(len(v_parts)):
            v_parts[j] = v_parts[j] - mean
            sq_sum = sq_sum + jnp.sum(v_parts[j] * v_parts[j], axis=-1, keepdims=True)
            yield
        rstd = lax.rsqrt(sq_sum * (1.0 / e) + LN_EPS)
        for g0 in range(0, n_groups, groups_per_pass):
            wide = slice(g0 * gd, (g0 + groups_per_pass) * gd)
            u = gelu(_dot(xb, win_ref[:, wide]) + bin_ref[:, wide])
            part = v_parts[(g0 * gd) // SG_V_COLS]
            off = (g0 * gd) % SG_V_COLS
            v = (part[:, off:off + groups_per_pass * gd] * rstd * lng_ref[:, wide]
                 + lnb_ref[:, wide]).astype(BF16)
            for gi in range(groups_per_pass):
                g = g0 + gi
                ws = jnp.where(causal, ws_ref[g], jnp.zeros((), BF16))
                cols = slice(g * gd, (g + 1) * gd)
                v_cat = jnp.concatenate(
                    [v[n * SG_CHUNK:(n + 1) * SG_CHUNK, gi * gd:(gi + 1) * gd]
                     for n in range(chunks)], axis=1)
                mixed = _dot(ws, v_cat)
                for n in range(chunks):
                    sub = slice(n * SG_CHUNK, (n + 1) * SG_CHUNK)
                    gate = mixed[:, n * gd:(n + 1) * gd] + bs_ref[g]
                    lo = t * SG_SUBTILE + n * SG_CHUNK
                    gated_ref[lo:lo + SG_CHUNK, cols] = (
                        u[sub, gi * gd:(gi + 1) * gd] * gate).astype(BF16)
            yield
        h = _dot(gated_ref[rows, :], wout_ref[...])
        yield
        o_ref[rows, :] = _layer_norm(alpha * x + h, lg_ref[...], lb_ref[...])

    _run_pipelined([subtile_steps(t) for t in range(tm // SG_SUBTILE)], SG_SKEW)


def _spatial_gating(x2, w_in, b_in, ln_g, ln_b, w_s, b_s, w_out, lg, lb, alpha, tm=1024):
    n, d = x2.shape
    e = w_out.shape[0]
    row_spec = pl.BlockSpec((tm, d), lambda i: (i, 0))
    consts = [w_in, b_in, ln_g, ln_b, w_s, b_s, w_out, lg, lb]
    return pl.pallas_call(
        functools.partial(_sg_kernel, alpha),
        grid=(n // tm,),
        in_specs=[row_spec] + [_const_spec(c.shape) for c in consts],
        out_specs=row_spec,
        out_shape=jax.ShapeDtypeStruct((n, d), F32),
        scratch_shapes=[pltpu.VMEM((tm, e), BF16)],
        compiler_params=_params(1),
        name="spatial_gating",
    )(x2, *consts)


def _ffn_kernel(alpha, f_chunk, has_mixer_out, *refs):
    if has_mixer_out:
        (x_ref, y_ref, g_ref, wo_ref, lg1_ref, lb1_ref,
         wgu_ref, wd_ref, lg_ref, lb_ref, o_ref, acc_ref) = refs
        h = _dot((y_ref[...] * g_ref[...]).astype(BF16), wo_ref[...])
        x = _layer_norm(alpha * x_ref[...] + h, lg1_ref[...], lb1_ref[...])
    else:
        x_ref, wgu_ref, wd_ref, lg_ref, lb_ref, o_ref, acc_ref = refs
        x = x_ref[...]
    xb = x.astype(BF16)
    f = wd_ref.shape[0]
    for ci in range(f // f_chunk):
        lo = ci * f_chunk
        gate = _dot(xb, wgu_ref[:, lo:lo + f_chunk])
        up = _dot(xb, wgu_ref[:, f + lo:f + lo + f_chunk])
        act = (gate * _sigmoid(gate) * up).astype(BF16)
        part = _dot(act, wd_ref[lo:lo + f_chunk, :])
        if ci == 0:
            acc_ref[...] = part
        else:
            acc_ref[...] += part
    o_ref[...] = _layer_norm(alpha * x + acc_ref[...], lg_ref[...], lb_ref[...])


def _ffn(x2, mixer_out, w_gate_up, w_down, lg, lb, alpha, tm=512, f_chunk=256):
    n, d = x2.shape
    row_spec = pl.BlockSpec((tm, d), lambda i: (i, 0))
    rows = [x2]
    consts = [w_gate_up, w_down, lg, lb]
    if mixer_out is not None:
        rows += list(mixer_out[:2])
        consts = list(mixer_out[2:]) + consts
    return pl.pallas_call(
        functools.partial(_ffn_kernel, alpha, f_chunk, mixer_out is not None),
        grid=(n // tm,),
        in_specs=[row_spec] * len(rows) + [_const_spec(c.shape) for c in consts],
        out_specs=row_spec,
        out_shape=jax.ShapeDtypeStruct((n, d), F32),
        scratch_shapes=[pltpu.VMEM((tm, d), F32)],
        compiler_params=_params(1),
        name="swiglu",
    )(*rows, *consts)


def kernel(x, rw_mu, rw_w_rkv, rw_w0, rw_w1, rw_w2, rw_a0, rw_a1, rw_a2, rw_v0, rw_v1, rw_v2, rw_g1, rw_g2, rw_k_k, rw_k_a, rw_r_k, rw_lnx_g, rw_lnx_b, rw_w_out, sg_w_in, sg_b_in, sg_ln_g, sg_ln_b, sg_w_s, sg_b_s, sg_w_out, ff_w_gate_up, ff_w_down, ln_g, ln_b):
    batch, seq_len, d = x.shape
    depth = ff_w_gate_up.shape[0]
    alpha = (2.0 * depth) ** 0.25
    n = batch * seq_len
    x2 = x.reshape(n, d)
    bf = lambda w: w.astype(BF16)
    rowv = lambda p: p.reshape(1, -1)

    v_first = None
    for i in range(depth):
        j = i // 2
        lg, lb = rowv(ln_g[i, 0]), rowv(ln_b[i, 0])
        if i % 2 == 0:
            vres = None if j == 0 else (rowv(rw_v0[j - 1]), bf(rw_v1[j - 1]), bf(rw_v2[j - 1]))
            r, lw, k, v, al, g = _rwkv_pre(
                x2, v_first, seq_len, rw_mu[j], bf(rw_w_rkv[j]), rowv(rw_w0[j]), bf(rw_w1[j]),
                bf(rw_w2[j]), rowv(rw_a0[j]), bf(rw_a1[j]), bf(rw_a2[j]), vres,
                bf(rw_g1[j]), bf(rw_g2[j]))
            if j == 0:
                v_first = v
            y = _wkv(r, lw, k, v, al, rowv(rw_k_k[j]), rowv(rw_k_a[j]), rowv(rw_r_k[j]),
                     rowv(rw_lnx_g[j]), rowv(rw_lnx_b[j]), batch, seq_len)
            mixer_out = (y, g, bf(rw_w_out[j]), lg, lb)
        else:
            mixer_out = None
            n_groups, chunk = sg_b_s.shape[1], sg_b_s.shape[2]
            bias = jnp.broadcast_to(sg_b_s[j][:, :, None], (n_groups, chunk, LANES))
            x2 = _spatial_gating(
                x2, bf(sg_w_in[j]), rowv(sg_b_in[j]), rowv(sg_ln_g[j]), rowv(sg_ln_b[j]),
                bf(sg_w_s[j]), bias, bf(sg_w_out[j]), lg, lb, alpha)
        x2 = _ffn(x2, mixer_out, bf(ff_w_gate_up[i]), bf(ff_w_down[i]),
                  rowv(ln_g[i, 1]), rowv(ln_b[i, 1]), alpha)
    return x2.reshape(batch, seq_len, d)
```
